```python
import math
import jax, jax.numpy as jnp
from jax import lax
import numpy as np

D_MODEL = 1024
BATCH = 2
SEQ = 8192
DEPTH = 4

N_MIXERS = 3
D_FF = 2816
LN_EPS = 1e-5
DEEPNORM_ALPHA = (2 * DEPTH) ** 0.25
DEEPNORM_BETA = (8 * DEPTH) ** -0.25
MACARON_WEIGHT = 0.5

SSM_EXPAND = 2
SSM_D_INNER = SSM_EXPAND * D_MODEL
SSM_HEAD_DIM = 64
SSM_HEADS = SSM_D_INNER // SSM_HEAD_DIM
SSM_GROUPS = 4
SSM_HEADS_PER_GROUP = SSM_HEADS // SSM_GROUPS
SSM_D_STATE = 128
SSM_CONV = 4
SSM_CHUNK = 256
SSM_CONV_DIM = SSM_D_INNER + 2 * SSM_GROUPS * SSM_D_STATE
SSM_IN_DIM = 2 * SSM_D_INNER + 2 * SSM_GROUPS * SSM_D_STATE + SSM_HEADS

ATT_HEAD_DIM = 64
FOX_HEADS = D_MODEL // ATT_HEAD_DIM
FOX_Q_BLOCK = 128
MOBA_HEADS = D_MODEL // ATT_HEAD_DIM
MOBA_BLOCK = 256
MOBA_TOPK = 3
MOBA_Q_BLOCK = 32

N_SSM_LAYERS = (DEPTH + 2) // 3
N_FOX_LAYERS = (DEPTH + 1) // 3
N_MOBA_LAYERS = DEPTH // 3

kernel_name = 'hybrid_ssd_fox_moba_macaron_deepnorm'


def layer_norm(x, g, b):
    xf = x.astype(jnp.float32)
    mu = jnp.mean(xf, -1, keepdims=True)
    var = jnp.mean(jnp.square(xf - mu), -1, keepdims=True)
    return ((xf - mu) * lax.rsqrt(var + LN_EPS) * g + b).astype(x.dtype)


def swiglu(x, w_gate, w_up, w_down):
    return (jax.nn.silu(x @ w_gate) * (x @ w_up)) @ w_down


def segsum(a):
    T = a.shape[-1]
    cs = jnp.cumsum(a, axis=-1)
    diff = cs[..., :, None] - cs[..., None, :]
    return jnp.where(jnp.tril(jnp.ones((T, T), dtype=bool)), diff, -jnp.inf)


def ssd_chunked(xdt, adt, b, c):
    Bsz, S, G, R, P = xdt.shape
    N = b.shape[-1]
    Q = SSM_CHUNK
    nc = -(-S // Q)
    pad = nc * Q - S
    padseq = lambda t: jnp.pad(t, [(0, 0), (0, pad)] + [(0, 0)] * (t.ndim - 2))
    xdt, adt, b, c = padseq(xdt), padseq(adt), padseq(b), padseq(c)
    xdt = xdt.reshape(Bsz, nc, Q, G, R, P)
    b = b.reshape(Bsz, nc, Q, G, N)
    c = c.reshape(Bsz, nc, Q, G, N)
    a_t = adt.reshape(Bsz, nc, Q, G, R).transpose(0, 1, 3, 4, 2)
    a_cs = jnp.cumsum(a_t, axis=-1)
    decay_in = jnp.exp(segsum(a_t))
    cb = jnp.einsum('bclgn,bcsgn->bcgls', c, b)
    m = cb[:, :, :, None] * decay_in
    y_diag = jnp.einsum('bcgrls,bcsgrp->bclgrp', m, xdt)
    decay_to_end = jnp.exp(a_cs[..., -1:] - a_cs).transpose(0, 1, 4, 2, 3)
    states = jnp.einsum('bcsgn,bcsgrp->bcgrpn', b, xdt * decay_to_end[..., None])
    states = jnp.concatenate([jnp.zeros_like(states[:, :1]), states], axis=1)
    a_last = jnp.pad(a_cs[..., -1].transpose(0, 2, 3, 1), [(0, 0), (0, 0), (0, 0), (1, 0)])
    chunk_decay = jnp.exp(segsum(a_last))
    states = jnp.einsum('bgrzc,bcgrpn->bzgrpn', chunk_decay, states)[:, :-1]
    decay_out = jnp.exp(a_cs).transpose(0, 1, 4, 2, 3)
    y_off = jnp.einsum('bclgn,bcgrpn->bclgrp', c, states) * decay_out[..., None]
    return (y_diag + y_off).reshape(Bsz, nc * Q, G, R, P)[:, :S]


def mamba2_mixer(x, w_in, conv_w, conv_b, dt_bias, a_log, d_skip, norm_w, w_out):
    Bsz, S, _ = x.shape
    G, R, P, N = SSM_GROUPS, SSM_HEADS_PER_GROUP, SSM_HEAD_DIM, SSM_D_STATE
    f32 = jnp.float32
    zxbcdt = x @ w_in
    z, xbc, dt = jnp.split(zxbcdt, [SSM_D_INNER, SSM_D_INNER + SSM_CONV_DIM], axis=-1)
    xbc = lax.conv_general_dilated(
        xbc, conv_w[:, None, :], window_strides=(1,), padding=[(SSM_CONV - 1, 0)],
        dimension_numbers=('NWC', 'WIO', 'NWC'), feature_group_count=SSM_CONV_DIM) + conv_b
    xbc = jax.nn.silu(xbc)
    xs, b_ssm, c_ssm = jnp.split(xbc, [SSM_D_INNER, SSM_D_INNER + G * N], axis=-1)
    dt = jax.nn.softplus(dt.astype(f32) + dt_bias.astype(f32))
    a = -jnp.exp(a_log.astype(f32))
    xh = xs.astype(f32).reshape(Bsz, S, G, R, P)
    bg = b_ssm.astype(f32).reshape(Bsz, S, G, N)
    cg = c_ssm.astype(f32).reshape(Bsz, S, G, N)
    y = ssd_chunked(xh * dt.reshape(Bsz, S, G, R)[..., None], (dt * a).reshape(Bsz, S, G, R), bg, cg)
    y = y + xh * d_skip.astype(f32).reshape(G, R)[:, :, None]
    y = y.reshape(Bsz, S, SSM_D_INNER) * jax.nn.silu(z.astype(f32))
    yg = y.reshape(Bsz, S, G, SSM_D_INNER // G)
    yg = yg * lax.rsqrt(jnp.mean(jnp.square(yg), -1, keepdims=True) + LN_EPS)
    y = yg.reshape(Bsz, S, SSM_D_INNER) * norm_w
    return y.astype(x.dtype) @ w_out


def fox_attention(x, w_in, b_f, w_out):
    Bsz, S, _ = x.shape
    H, Dh, QB = FOX_HEADS, ATT_HEAD_DIM, FOX_Q_BLOCK
    f32 = jnp.float32
    proj = x @ w_in
    q, k, v, f_logit = jnp.split(proj, [H * Dh, 2 * H * Dh, 3 * H * Dh], axis=-1)
    to_heads = lambda t: t.reshape(Bsz, S, H, Dh).transpose(0, 2, 1, 3).astype(f32)
    q, k, v = to_heads(q), to_heads(k), to_heads(v)
    log_f = jax.nn.log_sigmoid(f_logit.astype(f32) + b_f.astype(f32))
    cum = jnp.cumsum(log_f, axis=1).transpose(0, 2, 1)
    scale = Dh ** -0.5
    kpos = jnp.arange(S)

    def block(i):
        start = i * QB
        qb = lax.dynamic_slice_in_dim(q, start, QB, axis=2)
        cq = lax.dynamic_slice_in_dim(cum, start, QB, axis=2)
        logits = jnp.einsum('bhqd,bhkd->bhqk', qb, k) * scale + (cq[..., :, None] - cum[..., None, :])
        qpos = start + jnp.arange(QB)
        logits = jnp.where(kpos[None, :] <= qpos[:, None], logits, -jnp.inf)
        p = jax.nn.softmax(logits, axis=-1)
        return jnp.einsum('bhqk,bhkd->bhqd', p, v)

    o = lax.map(block, jnp.arange(S // QB))
    o = o.transpose(1, 0, 3, 2, 4).reshape(Bsz, S, H * Dh)
    return o.astype(x.dtype) @ w_out


def moba_attention(x, w_in, w_out):
    Bsz, S, _ = x.shape
    H, Dh, L, QB = MOBA_HEADS, ATT_HEAD_DIM, MOBA_BLOCK, MOBA_Q_BLOCK
    f32 = jnp.float32
    proj = x @ w_in
    q, k, v = jnp.split(proj, [H * Dh, 2 * H * Dh], axis=-1)
    to_heads = lambda t: t.reshape(Bsz, S, H, Dh).transpose(0, 2, 1, 3).astype(f32)
    q, k, v = to_heads(q), to_heads(k), to_heads(v)
    nb = -(-S // L)
    pad = nb * L - S
    k_blk = jnp.pad(k, [(0, 0), (0, 0), (0, pad), (0, 0)]).reshape(Bsz, H, nb, L, Dh)
    v_blk = jnp.pad(v, [(0, 0), (0, 0), (0, pad), (0, 0)]).reshape(Bsz, H, nb, L, Dh)
    k_mean = jnp.mean(k_blk, axis=3)
    k_sel_n = min(MOBA_TOPK, nb)
    scale = Dh ** -0.5
    bi = jnp.arange(Bsz)[:, None, None, None]
    hi = jnp.arange(H)[None, :, None, None]
    blk_ids = jnp.arange(nb)

    def chunk(i):
        start = i * QB
        own = start // L
        qc = lax.dynamic_slice_in_dim(q, start, QB, axis=2)
        gate = jnp.einsum('bhqd,bhnd->bhqn', qc, k_mean)
        gate = jnp.where(blk_ids < own, gate, -jnp.inf)
        _, idx = lax.top_k(gate, k_sel_n)
        valid = jnp.arange(k_sel_n) < own
        k_sel = k_blk[bi, hi, idx]
        v_sel = v_blk[bi, hi, idx]
        s_past = jnp.einsum('bhqd,bhqnld->bhqnl', qc, k_sel) * scale
        s_past = jnp.where(valid[:, None], s_past, -jnp.inf).reshape(Bsz, H, QB, k_sel_n * L)
        k_own = lax.dynamic_index_in_dim(k_blk, own, axis=2, keepdims=False)
        v_own = lax.dynamic_index_in_dim(v_blk, own, axis=2, keepdims=False)
        s_own = jnp.einsum('bhqd,bhld->bhql', qc, k_own) * scale
        qpos = start + jnp.arange(QB)
        kpos = own * L + jnp.arange(L)
        s_own = jnp.where(kpos[None, :] <= qpos[:, None], s_own, -jnp.inf)
        p = jax.nn.softmax(jnp.concatenate([s_past, s_own], axis=-1), axis=-1)
        o = jnp.einsum('bhqm,bhqmd->bhqd', p[..., :k_sel_n * L], v_sel.reshape(Bsz, H, QB, k_sel_n * L, Dh))
        return o + jnp.einsum('bhql,bhld->bhqd', p[..., k_sel_n * L:], v_own)

    o = lax.map(chunk, jnp.arange(S // QB))
    o = o.transpose(1, 0, 3, 2, 4).reshape(Bsz, S, H * Dh)
    return o.astype(x.dtype) @ w_out


def setup_inputs(seed: int = 0) -> dict:
    key = jax.random.key(seed)
    ks = jax.random.split(key, 20)
    f32 = jnp.float32
    nrm = lambda k, shape, s: jax.random.normal(k, shape, f32) * s
    x = nrm(ks[0], (BATCH, SEQ, D_MODEL), 1.0)
    ffn_w_gate = nrm(ks[1], (DEPTH, 2, D_MODEL, D_FF), D_MODEL ** -0.5)
    ffn_w_up = nrm(ks[2], (DEPTH, 2, D_MODEL, D_FF), D_MODEL ** -0.5)
    ffn_w_down = nrm(ks[3], (DEPTH, 2, D_FF, D_MODEL), D_FF ** -0.5 * DEEPNORM_BETA)
    ln_g = 1.0 + nrm(ks[4], (DEPTH, 3, D_MODEL), 0.02)
    ln_b = nrm(ks[5], (DEPTH, 3, D_MODEL), 0.02)
    ssm_w_in = nrm(ks[6], (N_SSM_LAYERS, D_MODEL, SSM_IN_DIM), D_MODEL ** -0.5)
    ssm_conv_w = nrm(ks[7], (N_SSM_LAYERS, SSM_CONV, SSM_CONV_DIM), SSM_CONV ** -0.5)
    ssm_conv_b = nrm(ks[8], (N_SSM_LAYERS, SSM_CONV_DIM), 0.02)
    dt0 = jnp.exp(jax.random.uniform(ks[9], (N_SSM_LAYERS, SSM_HEADS), f32, math.log(1e-3), math.log(1e-1)))
    ssm_dt_bias = dt0 + jnp.log(-jnp.expm1(-dt0))
    ssm_a_log = jnp.log(jax.random.uniform(ks[10], (N_SSM_LAYERS, SSM_HEADS), f32, 1.0, 16.0))
    ssm_d = 1.0 + nrm(ks[11], (N_SSM_LAYERS, SSM_HEADS), 0.1)
    ssm_norm_w = 1.0 + nrm(ks[12], (N_SSM_LAYERS, SSM_D_INNER), 0.02)
    ssm_w_out = nrm(ks[13], (N_SSM_LAYERS, SSM_D_INNER, D_MODEL), SSM_D_INNER ** -0.5 * DEEPNORM_BETA)
    fox_w_in = nrm(ks[14], (N_FOX_LAYERS, D_MODEL, 3 * FOX_HEADS * ATT_HEAD_DIM + FOX_HEADS), D_MODEL ** -0.5)
    fox_b_f = jax.random.uniform(ks[15], (N_FOX_LAYERS, FOX_HEADS), f32, 1.0, 6.0)
    fox_w_out = nrm(ks[16], (N_FOX_LAYERS, FOX_HEADS * ATT_HEAD_DIM, D_MODEL), (FOX_HEADS * ATT_HEAD_DIM) ** -0.5 * DEEPNORM_BETA)
    moba_w_in = nrm(ks[17], (N_MOBA_LAYERS, D_MODEL, 3 * MOBA_HEADS * ATT_HEAD_DIM), D_MODEL ** -0.5)
    moba_w_out = nrm(ks[18], (N_MOBA_LAYERS, MOBA_HEADS * ATT_HEAD_DIM, D_MODEL), (MOBA_HEADS * ATT_HEAD_DIM) ** -0.5 * DEEPNORM_BETA)
    return {'x': x, 'ffn_w_gate': ffn_w_gate, 'ffn_w_up': ffn_w_up, 'ffn_w_down': ffn_w_down,
            'ln_g': ln_g, 'ln_b': ln_b,
            'ssm_w_in': ssm_w_in, 'ssm_conv_w': ssm_conv_w, 'ssm_conv_b': ssm_conv_b,
            'ssm_dt_bias': ssm_dt_bias, 'ssm_a_log': ssm_a_log, 'ssm_d': ssm_d,
            'ssm_norm_w': ssm_norm_w, 'ssm_w_out': ssm_w_out,
            'fox_w_in': fox_w_in, 'fox_b_f': fox_b_f, 'fox_w_out': fox_w_out,
            'moba_w_in': moba_w_in, 'moba_w_out': moba_w_out}


def reference(x, ffn_w_gate, ffn_w_up, ffn_w_down, ln_g, ln_b,
              ssm_w_in, ssm_conv_w, ssm_conv_b, ssm_dt_bias, ssm_a_log, ssm_d, ssm_norm_w, ssm_w_out,
              fox_w_in, fox_b_f, fox_w_out, moba_w_in, moba_w_out):
    h = x
    for layer in range(DEPTH):
        kind, j = layer % N_MIXERS, layer // N_MIXERS
        ff = swiglu(h, ffn_w_gate[layer, 0], ffn_w_up[layer, 0], ffn_w_down[layer, 0])
        h = layer_norm(DEEPNORM_ALPHA * h + MACARON_WEIGHT * ff, ln_g[layer, 0], ln_b[layer, 0])
        if kind == 0:
            mix = mamba2_mixer(h, ssm_w_in[j], ssm_conv_w[j], ssm_conv_b[j], ssm_dt_bias[j],
                               ssm_a_log[j], ssm_d[j], ssm_norm_w[j], ssm_w_out[j])
        elif kind == 1:
            mix = fox_attention(h, fox_w_in[j], fox_b_f[j], fox_w_out[j])
        else:
            mix = moba_attention(h, moba_w_in[j], moba_w_out[j])
        h = layer_norm(DEEPNORM_ALPHA * h + mix, ln_g[layer, 1], ln_b[layer, 1])
        ff = swiglu(h, ffn_w_gate[layer, 1], ffn_w_up[layer, 1], ffn_w_down[layer, 1])
        h = layer_norm(DEEPNORM_ALPHA * h + MACARON_WEIGHT * ff, ln_g[layer, 2], ln_b[layer, 2])
    return h
```

```python
import functools

import jax
import jax.numpy as jnp
from jax import lax
from jax.experimental import pallas as pl
from jax.experimental.pallas import tpu as pltpu

F32 = jnp.float32
BF16 = jnp.bfloat16

DEPTH = 4
D_MODEL = 1024
D_FF = 2816
LN_EPS = 1e-5
DEEPNORM_ALPHA = (2 * DEPTH) ** 0.25
MACARON_WEIGHT = 0.5

SSM_D_INNER = 2048
SSM_HEAD_DIM = 64
SSM_HEADS = 32
SSM_GROUPS = 4
SSM_HEADS_PER_GROUP = 8
SSM_D_STATE = 128
SSM_CONV = 4
SSM_CHUNK = 256
SSM_BC_DIM = SSM_GROUPS * SSM_D_STATE
SSM_CONV_DIM = SSM_D_INNER + 2 * SSM_BC_DIM
SSM_GROUP_WIDTH = SSM_D_INNER // SSM_GROUPS

ATT_HEAD_DIM = 64
ATT_HEADS = 16
ATT_SCALE = ATT_HEAD_DIM ** -0.5
MOBA_BLOCK = 256
MOBA_TOPK = 3

LANES = 128
NEG_INF = float("-inf")
VMEM_LIMIT = 56 * 1024 * 1024


def _params(*sem):
    return pltpu.CompilerParams(dimension_semantics=sem, vmem_limit_bytes=VMEM_LIMIT)


def _const_spec(shape):
    nd = len(shape)
    return pl.BlockSpec(shape, lambda *_: (0,) * nd, pipeline_mode=pl.Buffered(1))


def _layer_norm(v, g, b):
    mu = jnp.mean(v, -1, keepdims=True)
    d = v - mu
    var = jnp.mean(d * d, -1, keepdims=True)
    return d * lax.rsqrt(var + LN_EPS) * g + b


def _split3(v):
    hi = v.astype(BF16)
    r1 = v - hi.astype(F32)
    mid = r1.astype(BF16)
    lo = (r1 - mid.astype(F32)).astype(BF16)
    return hi, mid, lo


def _dot3(v, m01, dims=None):
    if dims is None:
        dims = (((v.ndim - 1,), (0,)), ((), ()))
    out = None
    for part in _split3(v):
        t = lax.dot_general(part, m01, dims, preferred_element_type=F32)
        out = t if out is None else out + t
    return out


def _softplus(x):
    return jnp.maximum(x, 0.0) + jnp.log1p(jnp.exp(-jnp.abs(x)))


def _silu(x):
    return x * jax.nn.sigmoid(x)


def _ffn_kernel(h_ref, wg_ref, wu_ref, wd_ref, g_ref, b_ref, o_ref):
    h = h_ref[...]
    hb = h.astype(BF16)
    gate = jnp.dot(hb, wg_ref[...], preferred_element_type=F32)
    up = jnp.dot(hb, wu_ref[...], preferred_element_type=F32)
    act = (_silu(gate) * up).astype(BF16)
    ff = jnp.dot(act, wd_ref[...], preferred_element_type=F32)
    o_ref[...] = _layer_norm(DEEPNORM_ALPHA * h + MACARON_WEIGHT * ff, g_ref[...], b_ref[...])


def _ffn_ln(h, wg, wu, wd, g, b, tm=256):
    T, D = h.shape
    FF = wg.shape[1]
    return pl.pallas_call(
        _ffn_kernel,
        grid=(T // tm,),
        in_specs=[
            pl.BlockSpec((tm, D), lambda i: (i, 0)),
            _const_spec((D, FF)), _const_spec((D, FF)), _const_spec((FF, D)),
            _const_spec((1, D)), _const_spec((1, D)),
        ],
        out_specs=pl.BlockSpec((tm, D), lambda i: (i, 0)),
        out_shape=jax.ShapeDtypeStruct((T, D), F32),
        compiler_params=_params("parallel"),
        name="ffn_ln",
    )(h, wg, wu, wd, g.reshape(1, D), b.reshape(1, D))


def _proj_kernel(x_ref, w_ref, o_ref):
    o_ref[...] = jnp.dot(x_ref[...].astype(BF16), w_ref[...],
                         preferred_element_type=F32).astype(o_ref.dtype)


def _proj(x, w, out_dtype, tm=512):
    T, D = x.shape
    N = w.shape[1]
    return pl.pallas_call(
        _proj_kernel,
        grid=(T // tm,),
        in_specs=[pl.BlockSpec((tm, D), lambda i: (i, 0)), _const_spec((D, N))],
        out_specs=pl.BlockSpec((tm, N), lambda i: (i, 0)),
        out_shape=jax.ShapeDtypeStruct((T, N), out_dtype),
        compiler_params=_params("parallel"),
        name="proj",
    )(x, w)


def _outproj_kernel(y_ref, w_ref, h_ref, g_ref, b_ref, o_ref):
    mix = jnp.dot(y_ref[...].astype(BF16), w_ref[...], preferred_element_type=F32)
    o_ref[...] = _layer_norm(DEEPNORM_ALPHA * h_ref[...] + mix, g_ref[...], b_ref[...])


def _outproj_ln(y, w, h, g, b, tm=512):
    T, K = y.shape
    D = w.shape[1]
    return pl.pallas_call(
        _outproj_kernel,
        grid=(T // tm,),
        in_specs=[
            pl.BlockSpec((tm, K), lambda i: (i, 0)), _const_spec((K, D)),
            pl.BlockSpec((tm, D), lambda i: (i, 0)),
            _const_spec((1, D)), _const_spec((1, D)),
        ],
        out_specs=pl.BlockSpec((tm, D), lambda i: (i, 0)),
        out_shape=jax.ShapeDtypeStruct((T, D), F32),
        compiler_params=_params("parallel"),
        name="outproj_ln",
    )(y, w, h, g.reshape(1, D), b.reshape(1, D))


def _ssd_kernel(z_ref, xbc_ref, dt_ref, cw_ref, cb_ref, dtb_ref, alog_ref, dskip_ref, nw_ref,
                y_ref, pad_ref, state_ref, ydiag_ref):
    Q = SSM_CHUNK
    c = pl.program_id(1)

    @pl.when(c == 0)
    def _():
        pad_ref[0:8, :] = jnp.zeros((8, SSM_CONV_DIM), F32)
        state_ref[...] = jnp.zeros_like(state_ref)

    xraw = xbc_ref[0]
    pad_ref[8:8 + Q, :] = xraw
    conv = cb_ref[...] + cw_ref[SSM_CONV - 1:SSM_CONV, :] * xraw
    for k in range(1, SSM_CONV):
        conv = conv + cw_ref[SSM_CONV - 1 - k:SSM_CONV - k, :] * pad_ref[pl.ds(8 - k, Q), :]
    pad_ref[0:8, :] = xraw[Q - 8:Q, :]
    xbc = _silu(conv)
    xs = xbc[:, :SSM_D_INNER]

    dt = _softplus(dt_ref[0] + dtb_ref[...])
    adt = dt * (-jnp.exp(alog_ref[...]))
    row = lax.broadcasted_iota(jnp.int32, (Q, Q), 0)
    col = lax.broadcasted_iota(jnp.int32, (Q, Q), 1)
    tril = row >= col
    cs = _dot3_left(tril.astype(BF16), adt)
    cs_t = cs.T
    cs_last = cs[Q - 1:Q, :]

    eh = lax.broadcasted_iota(jnp.int32, (LANES, SSM_D_INNER), 0)
    ec = lax.broadcasted_iota(jnp.int32, (LANES, SSM_D_INNER), 1)
    expand = (ec // SSM_HEAD_DIM == eh).astype(BF16)
    xdt = xs * _dot3(dt, expand)
    dec_out = _dot3(jnp.exp(cs), expand)
    dec_end = _dot3(jnp.exp(cs_last - cs), expand)
    dec_chunk = _dot3(jnp.broadcast_to(jnp.exp(cs_last), (8, LANES)), expand)[0:1, :]

    xdt_b = xdt.astype(BF16)
    xend_b = (xdt * dec_end).astype(BF16)
    lane = lax.broadcasted_iota(jnp.int32, (Q, LANES), 1)
    low_half = lane < SSM_HEAD_DIM

    for g in range(SSM_GROUPS):
        bg = xbc[:, SSM_D_INNER + g * SSM_D_STATE:SSM_D_INNER + (g + 1) * SSM_D_STATE]
        cg = xbc[:, SSM_D_INNER + SSM_BC_DIM + g * SSM_D_STATE:
                 SSM_D_INNER + SSM_BC_DIM + (g + 1) * SSM_D_STATE]
        bg_b = bg.astype(BF16)
        cg_b = cg.astype(BF16)
        cb = lax.dot_general(cg_b, bg_b, (((1,), (1,)), ((), ())), preferred_element_type=F32)
        for pr in range(SSM_HEADS_PER_GROUP // 2):
            pair = g * (SSM_HEADS_PER_GROUP // 2) + pr
            xp = xdt_b[:, pair * LANES:(pair + 1) * LANES]
            acc = None
            for e in range(2):
                h = 2 * pair + e
                seg = cs[:, h:h + 1] - cs_t[h:h + 1, :]
                m = (cb * jnp.exp(jnp.where(tril, seg, NEG_INF))).astype(BF16)
                xm = jnp.where(low_half if e == 0 else jnp.logical_not(low_half), xp, jnp.zeros_like(xp))
                t = jnp.dot(m, xm, preferred_element_type=F32)
                acc = t if acc is None else acc + t
            ydiag_ref[:, pair * LANES:(pair + 1) * LANES] = acc
        gs = slice(g * SSM_GROUP_WIDTH, (g + 1) * SSM_GROUP_WIDTH)
        st = state_ref[g]
        y_off = jnp.dot(cg_b, st.astype(BF16), preferred_element_type=F32) * dec_out[:, gs]
        ydiag_ref[:, gs] = ydiag_ref[:, gs] + y_off
        upd = jnp.dot(bg.T.astype(BF16), xend_b[:, gs], preferred_element_type=F32)
        state_ref[g] = st * dec_chunk[:, gs] + upd

    y = ydiag_ref[...] + xs * dskip_ref[...]
    y = y * _silu(z_ref[0])
    for g in range(SSM_GROUPS):
        gs = slice(g * SSM_GROUP_WIDTH, (g + 1) * SSM_GROUP_WIDTH)
        yg = y[:, gs]
        yg = yg * lax.rsqrt(jnp.mean(yg * yg, -1, keepdims=True) + LN_EPS)
        y_ref[0, :, gs] = (yg * nw_ref[:, gs]).astype(y_ref.dtype)


def _dot3_left(m01, v):
    out = None
    for part in _split3(v):
        t = jnp.dot(m01, part, preferred_element_type=F32)
        out = t if out is None else out + t
    return out


def _pad_lanes(v, n=LANES):
    return jnp.pad(v, [(0, 0)] * (v.ndim - 1) + [(0, n - v.shape[-1])])


def _ssd_core(z, xbc, dt, conv_w, conv_b, dt_bias, a_log, d_skip, norm_w):
    B, S, _ = z.shape
    Q = SSM_CHUNK
    nc = S // Q
    blk = lambda w: pl.BlockSpec((1, Q, w), lambda b, c: (b, c, 0))
    return pl.pallas_call(
        _ssd_kernel,
        grid=(B, nc),
        in_specs=[
            blk(SSM_D_INNER), blk(SSM_CONV_DIM), blk(LANES),
            _const_spec((SSM_CONV, SSM_CONV_DIM)), _const_spec((1, SSM_CONV_DIM)),
            _const_spec((1, LANES)), _const_spec((1, LANES)),
            _const_spec((1, SSM_D_INNER)), _const_spec((1, SSM_D_INNER)),
        ],
        out_specs=blk(SSM_D_INNER),
        out_shape=jax.ShapeDtypeStruct((B, S, SSM_D_INNER), BF16),
        scratch_shapes=[
            pltpu.VMEM((8 + Q, SSM_CONV_DIM), F32),
            pltpu.VMEM((SSM_GROUPS, SSM_D_STATE, SSM_GROUP_WIDTH), F32),
            pltpu.VMEM((Q, SSM_D_INNER), F32),
        ],
        compiler_params=_params("arbitrary", "arbitrary"),
        name="ssd_core",
    )(z, xbc, dt, conv_w, conv_b.reshape(1, -1), _pad_lanes(dt_bias.reshape(1, -1)),
      _pad_lanes(a_log.reshape(1, -1)),
      jnp.repeat(d_skip, SSM_HEAD_DIM).reshape(1, -1), norm_w.reshape(1, -1))


def _mamba2_mixer(h, B, S, w_in, conv_w, conv_b, dt_bias, a_log, d_skip, norm_w, w_out, ln_g, ln_b):
    wz = w_in[:, :SSM_D_INNER].astype(BF16)
    wx = w_in[:, SSM_D_INNER:SSM_D_INNER + SSM_CONV_DIM].astype(BF16)
    wdt = _pad_lanes(w_in[:, SSM_D_INNER + SSM_CONV_DIM:]).astype(BF16)
    z = _proj(h, wz, F32).reshape(B, S, -1)
    xbc = _proj(h, wx, F32).reshape(B, S, -1)
    dt = _proj(h, wdt, F32).reshape(B, S, -1)
    y = _ssd_core(z, xbc, dt, conv_w, conv_b, dt_bias, a_log, d_skip, norm_w)
    return _outproj_ln(y.reshape(B * S, -1), w_out.astype(BF16), h, ln_g, ln_b)


def _cum_kernel(f_ref, bf_ref, o_ref, carry_ref):
    tc = f_ref.shape[1]

    @pl.when(pl.program_id(1) == 0)
    def _():
        carry_ref[...] = jnp.zeros_like(carry_ref)

    x = f_ref[0] + bf_ref[...]
    log_f = -_softplus(-x)
    row = lax.broadcasted_iota(jnp.int32, (tc, tc), 0)
    col = lax.broadcasted_iota(jnp.int32, (tc, tc), 1)
    cum = _dot3_left((row >= col).astype(BF16), log_f) + carry_ref[0:1, :]
    o_ref[0] = cum
    carry_ref[...] = jnp.broadcast_to(cum[tc - 1:tc, :], carry_ref.shape)


def _fox_cum(f, b_f, tc=256):
    B, S, W = f.shape
    return pl.pallas_call(
        _cum_kernel,
        grid=(B, S // tc),
        in_specs=[pl.BlockSpec((1, tc, W), lambda b, c: (b, c, 0)), _const_spec((1, W))],
        out_specs=pl.BlockSpec((1, tc, W), lambda b, c: (b, c, 0)),
        out_shape=jax.ShapeDtypeStruct((B, S, W), F32),
        scratch_shapes=[pltpu.VMEM((8, W), F32)],
        compiler_params=_params("arbitrary", "arbitrary"),
        name="fox_cum",
    )(f, _pad_lanes(b_f.reshape(1, -1), W))


def _online_softmax_step(s, v, m_ref, l_ref, acc_ref):
    m_old = m_ref[...]
    m_new = jnp.maximum(m_old, jnp.max(s, -1, keepdims=True))
    a = jnp.exp(m_old - m_new)
    p = jnp.exp(s - m_new)
    l_ref[...] = a * l_ref[...] + jnp.sum(p, -1, keepdims=True)
    acc_ref[...] = a * acc_ref[...] + jnp.dot(p.astype(BF16), v, preferred_element_type=F32)
    m_ref[...] = m_new


def _fox_kernel(q_ref, k_ref, v_ref, cq_ref, ck_ref, o_ref, m_ref, l_ref, acc_ref, *, tq):
    hp = pl.program_id(1)
    qi = pl.program_id(2)
    lane = lax.broadcasted_iota(jnp.int32, (tq, LANES), 1)
    half = [lane < ATT_HEAD_DIM, lane >= ATT_HEAD_DIM]
    q = q_ref[0]
    qs = q * jnp.asarray(ATT_SCALE, q.dtype)
    cq_all = cq_ref[0]
    row = lax.broadcasted_iota(jnp.int32, (tq, tq), 0)
    col = lax.broadcasted_iota(jnp.int32, (tq, tq), 1)
    causal = row >= col

    for e in range(2):
        m_ref[e] = jnp.full((tq, 1), NEG_INF, F32)
        l_ref[e] = jnp.zeros((tq, 1), F32)
        acc_ref[e] = jnp.zeros((tq, LANES), F32)
    qe = [jnp.where(half[e], qs, jnp.zeros_like(qs)) for e in range(2)]
    cq = [_pick_lane(cq_all, 2 * hp + e) for e in range(2)]

    def kv_step(j, diag):
        ks = pl.multiple_of(j * tq, tq)
        kt = k_ref[0, pl.ds(ks, tq), :]
        vt = v_ref[0, pl.ds(ks, tq), :]
        for e in range(2):
            s = lax.dot_general(qe[e], kt, (((1,), (1,)), ((), ())), preferred_element_type=F32)
            ck = ck_ref[0, pl.ds(2 * hp + e, 1), pl.ds(ks, tq)]
            s = s + (cq[e] - ck)
            if diag:
                s = jnp.where(causal, s, NEG_INF)
            _online_softmax_step(s, vt, m_ref.at[e], l_ref.at[e], acc_ref.at[e])

    def body(j, carry):
        kv_step(j, False)
        return carry

    lax.fori_loop(0, qi, body, 0)
    kv_step(qi, True)
    o0 = acc_ref[0] / l_ref[0]
    o1 = acc_ref[1] / l_ref[1]
    o_ref[0] = jnp.where(half[0], o0, o1).astype(o_ref.dtype)


def _pick_lane(x, idx):
    lane = lax.broadcasted_iota(jnp.int32, x.shape, 1)
    return jnp.sum(jnp.where(lane == idx, x, jnp.zeros_like(x)), -1, keepdims=True)


def _fox_attention_core(q, k, v, cum, cum_t, tq=256):
    B, S, D = q.shape
    H = cum_t.shape[1]
    kern = functools.partial(_fox_kernel, tq=tq)
    return pl.pallas_call(
        kern,
        grid=(B, D // LANES, S // tq),
        in_specs=[
            pl.BlockSpec((1, tq, LANES), lambda b, hp, i: (b, i, hp)),
            pl.BlockSpec((1, S, LANES), lambda b, hp, i: (b, 0, hp)),
            pl.BlockSpec((1, S, LANES), lambda b, hp, i: (b, 0, hp)),
            pl.BlockSpec((1, tq, H), lambda b, hp, i: (b, i, 0)),
            pl.BlockSpec((1, H, S), lambda b, hp, i: (b, 0, 0)),
        ],
        out_specs=pl.BlockSpec((1, tq, LANES), lambda b, hp, i: (b, i, hp)),
        out_shape=jax.ShapeDtypeStruct((B, S, D), BF16),
        scratch_shapes=[
            pltpu.VMEM((2, tq, 1), F32), pltpu.VMEM((2, tq, 1), F32),
            pltpu.VMEM((2, tq, LANES), F32),
        ],
        compiler_params=_params("parallel", "parallel", "arbitrary"),
        name="fox_attention",
    )(q, k, v, cum, cum_t)


def _fox_mixer(h, B, S, w_in, b_f, w_out, ln_g, ln_b):
    HD = ATT_HEADS * ATT_HEAD_DIM
    wq = w_in[:, :HD].astype(BF16)
    wk = w_in[:, HD:2 * HD].astype(BF16)
    wv = w_in[:, 2 * HD:3 * HD].astype(BF16)
    wf = _pad_lanes(w_in[:, 3 * HD:]).astype(BF16)
    q = _proj(h, wq, BF16).reshape(B, S, HD)
    k = _proj(h, wk, BF16).reshape(B, S, HD)
    v = _proj(h, wv, BF16).reshape(B, S, HD)
    f = _proj(h, wf, F32).reshape(B, S, LANES)
    cum = _fox_cum(f, b_f)[:, :, :ATT_HEADS]
    cum_t = jnp.transpose(cum, (0, 2, 1))
    o = _fox_attention_core(q, k, v, cum, cum_t)
    return _outproj_ln(o.reshape(B * S, HD), w_out.astype(BF16), h, ln_g, ln_b)


def _kmean_kernel(k_ref, o_ref):
    o_ref[0, 0] = jnp.mean(k_ref[0], axis=0, keepdims=True)


def _block_means(k):
    B, S, D = k.shape
    nb = S // MOBA_BLOCK
    return pl.pallas_call(
        _kmean_kernel,
        grid=(B, nb),
        in_specs=[pl.BlockSpec((1, MOBA_BLOCK, D), lambda b, n: (b, n, 0))],
        out_specs=pl.BlockSpec((1, 1, 1, D), lambda b, n: (b, n, 0, 0)),
        out_shape=jax.ShapeDtypeStruct((B, nb, 1, D), F32),
        compiler_params=_params("parallel", "parallel"),
        name="moba_kmean",
    )(k)


def _moba_kernel(qf_ref, km_ref, k_ref, v_ref, o_ref, m_ref, l_ref, acc_ref, sel_ref):
    L = MOBA_BLOCK
    own = pl.program_id(2)
    lane = lax.broadcasted_iota(jnp.int32, (L, LANES), 1)
    half = [lane < ATT_HEAD_DIM, lane >= ATT_HEAD_DIM]
    qf = qf_ref[0]
    qs = (qf * ATT_SCALE).astype(BF16)
    km = km_ref[0]
    row = lax.broadcasted_iota(jnp.int32, (L, L), 0)
    col = lax.broadcasted_iota(jnp.int32, (L, L), 1)
    causal = row >= col
    qse_all = [jnp.where(half[e], qs, jnp.zeros_like(qs)) for e in range(2)]

    for e in range(2):
        qe = jnp.where(half[e], qf, jnp.zeros_like(qf))
        gate = lax.dot_general(qe, km, (((1,), (1,)), ((), ())), preferred_element_type=F32,
                               precision=lax.Precision.HIGHEST)
        blk = lax.broadcasted_iota(jnp.int32, gate.shape, 1).astype(F32)
        gate = jnp.where(blk < own.astype(F32), gate, NEG_INF)
        sel = jnp.zeros(gate.shape, F32)
        for _ in range(MOBA_TOPK):
            mx = jnp.max(gate, -1, keepdims=True)
            first = jnp.min(jnp.where(gate == mx, blk, float(gate.shape[1])), -1, keepdims=True)
            pick = jnp.logical_and(blk == first, mx > NEG_INF)
            sel = jnp.where(pick, 1.0, sel)
            gate = jnp.where(pick, NEG_INF, gate)
        sel_ref[e] = sel

        qse = qse_all[e]
        ks = pl.multiple_of(own * L, L)
        kt = k_ref[0, pl.ds(ks, L), :]
        vt = v_ref[0, pl.ds(ks, L), :]
        s = lax.dot_general(qse, kt, (((1,), (1,)), ((), ())), preferred_element_type=F32)
        s = jnp.where(causal, s, NEG_INF)
        m0 = jnp.max(s, -1, keepdims=True)
        p = jnp.exp(s - m0)
        m_ref[e] = m0
        l_ref[e] = jnp.sum(p, -1, keepdims=True)
        acc_ref[e] = jnp.dot(p.astype(BF16), vt, preferred_element_type=F32)

    def body(j, carry):
        ks = pl.multiple_of(j * L, L)
        kt = k_ref[0, pl.ds(ks, L), :]
        vt = v_ref[0, pl.ds(ks, L), :]
        for e in range(2):
            qse = qse_all[e]
            s = lax.dot_general(qse, kt, (((1,), (1,)), ((), ())), preferred_element_type=F32)
            picked = _pick_lane(sel_ref[e], j) > 0.5
            s = jnp.where(picked, s, NEG_INF)
            _online_softmax_step(s, vt, m_ref.at[e], l_ref.at[e], acc_ref.at[e])
        return carry

    lax.fori_loop(0, own, body, 0)
    o0 = acc_ref[0] / l_ref[0]
    o1 = acc_ref[1] / l_ref[1]
    o_ref[0] = jnp.where(half[0], o0, o1).astype(o_ref.dtype)


def _moba_attention_core(qf, kmean, k, v):
    B, S, D = qf.shape
    L = MOBA_BLOCK
    nbp = kmean.shape[1]
    return pl.pallas_call(
        _moba_kernel,
        grid=(B, D // LANES, S // L),
        in_specs=[
            pl.BlockSpec((1, L, LANES), lambda b, hp, i: (b, i, hp)),
            pl.BlockSpec((1, nbp, LANES), lambda b, hp, i: (b, 0, hp)),
            pl.BlockSpec((1, S, LANES), lambda b, hp, i: (b, 0, hp)),
            pl.BlockSpec((1, S, LANES), lambda b, hp, i: (b, 0, hp)),
        ],
        out_specs=pl.BlockSpec((1, L, LANES), lambda b, hp, i: (b, i, hp)),
        out_shape=jax.ShapeDtypeStruct((B, S, D), BF16),
        scratch_shapes=[
            pltpu.VMEM((2, L, 1), F32), pltpu.VMEM((2, L, 1), F32),
            pltpu.VMEM((2, L, LANES), F32), pltpu.VMEM((2, L, nbp), F32),
        ],
        compiler_params=_params("parallel", "parallel", "arbitrary"),
        name="moba_attention",
    )(qf, kmean, k, v)


def _moba_mixer(h, B, S, w_in, w_out, ln_g, ln_b):
    HD = ATT_HEADS * ATT_HEAD_DIM
    wq = w_in[:, :HD].astype(BF16)
    wk = w_in[:, HD:2 * HD].astype(BF16)
    wv = w_in[:, 2 * HD:3 * HD].astype(BF16)
    qf = _proj(h, wq, F32).reshape(B, S, HD)
    kf = _proj(h, wk, F32).reshape(B, S, HD)
    v = _proj(h, wv, BF16).reshape(B, S, HD)
    nb = S // MOBA_BLOCK
    kmean = _block_means(kf).reshape(B, nb, HD)
    nbp = -(-nb // LANES) * LANES
    kmean = jnp.pad(kmean, [(0, 0), (0, nbp - nb), (0, 0)])
    o = _moba_attention_core(qf, kmean, kf.astype(BF16), v)
    return _outproj_ln(o.reshape(B * S, HD), w_out.astype(BF16), h, ln_g, ln_b)


def kernel(x, ffn_w_gate, ffn_w_up, ffn_w_down, ln_g, ln_b, ssm_w_in, ssm_conv_w, ssm_conv_b,
           ssm_dt_bias, ssm_a_log, ssm_d, ssm_norm_w, ssm_w_out, fox_w_in, fox_b_f, fox_w_out,
           moba_w_in, moba_w_out):
    B, S, D = x.shape
    h = x.reshape(B * S, D)
    for layer in range(DEPTH):
        kind, j = layer % 3, layer // 3
        h = _ffn_ln(h, ffn_w_gate[layer, 0].astype(BF16), ffn_w_up[layer, 0].astype(BF16),
                    ffn_w_down[layer, 0].astype(BF16), ln_g[layer, 0], ln_b[layer, 0])
        if kind == 0:
            h = _mamba2_mixer(h, B, S, ssm_w_in[j], ssm_conv_w[j], ssm_conv_b[j], ssm_dt_bias[j],
                              ssm_a_log[j], ssm_d[j], ssm_norm_w[j], ssm_w_out[j],
                              ln_g[layer, 1], ln_b[layer, 1])
        elif kind == 1:
            h = _fox_mixer(h, B, S, fox_w_in[j], fox_b_f[j], fox_w_out[j],
                           ln_g[layer, 1], ln_b[layer, 1])
        else:
            h = _moba_mixer(h, B, S, moba_w_in[j], moba_w_out[j], ln_g[layer, 1], ln_b[layer, 1])
        h = _ffn_ln(h, ffn_w_gate[layer, 1].astype(BF16), ffn_w_up[layer, 1].astype(BF16),
                    ffn_w_down[layer, 1].astype(BF16), ln_g[layer, 2], ln_b[layer, 2])
    return h.reshape(B, S, D)
```

```python
import functools

import jax
import jax.numpy as jnp
from jax import lax
from jax.experimental import pallas as pl
from jax.experimental.pallas import tpu as pltpu

F32 = jnp.float32
BF16 = jnp.bfloat16

DEPTH = 4
D_MODEL = 1024
D_FF = 2816
LN_EPS = 1e-5
DEEPNORM_ALPHA = (2 * DEPTH) ** 0.25
MACARON_WEIGHT = 0.5

SSM_D_INNER = 2048
SSM_HEAD_DIM = 64
SSM_HEADS = 32
SSM_GROUPS = 4
SSM_HEADS_PER_GROUP = 8
SSM_D_STATE = 128
SSM_CONV = 4
SSM_CHUNK = 256
SSM_BC_DIM = SSM_GROUPS * SSM_D_STATE
SSM_CONV_DIM = SSM_D_INNER + 2 * SSM_BC_DIM
SSM_GROUP_WIDTH = SSM_D_INNER // SSM_GROUPS

ATT_HEAD_DIM = 64
ATT_HEADS = 16
ATT_SCALE = ATT_HEAD_DIM ** -0.5
MOBA_BLOCK = 256
MOBA_TOPK = 3

LANES = 128
NEG_INF = float("-inf")
VMEM_LIMIT = 56 * 1024 * 1024


def _params(*sem):
    return pltpu.CompilerParams(dimension_semantics=sem, vmem_limit_bytes=VMEM_LIMIT)


def _const_spec(shape):
    nd = len(shape)
    return pl.BlockSpec(shape, lambda *_: (0,) * nd, pipeline_mode=pl.Buffered(1))


def _layer_norm(v, g, b):
    mu = jnp.mean(v, -1, keepdims=True)
    d = v - mu
    var = jnp.mean(d * d, -1, keepdims=True)
    return d * lax.rsqrt(var + LN_EPS) * g + b


def _split3(v):
    hi = v.astype(BF16)
    r1 = v - hi.astype(F32)
    mid = r1.astype(BF16)
    lo = (r1 - mid.astype(F32)).astype(BF16)
    return hi, mid, lo


def _dot3(v, m01):
    out = None
    for part in _split3(v):
        t = jnp.dot(part, m01, preferred_element_type=F32)
        out = t if out is None else out + t
    return out


def _dot3_left(m01, v):
    out = None
    for part in _split3(v):
        t = jnp.dot(m01, part, preferred_element_type=F32)
        out = t if out is None else out + t
    return out


def _softplus(x):
    return jnp.maximum(x, 0.0) + jnp.log1p(jnp.exp(-jnp.abs(x)))


def _silu(x):
    return x * jax.nn.sigmoid(x)


def _pad_lanes(v, n=LANES):
    return jnp.pad(v, [(0, 0)] * (v.ndim - 1) + [(0, n - v.shape[-1])])


def _ffn_kernel(h_ref, wg_ref, wu_ref, wd_ref, g_ref, b_ref, o_ref):
    h = h_ref[...]
    hb = h.astype(BF16)
    gate = jnp.dot(hb, wg_ref[...], preferred_element_type=F32)
    up = jnp.dot(hb, wu_ref[...], preferred_element_type=F32)
    act = (_silu(gate) * up).astype(BF16)
    ff = jnp.dot(act, wd_ref[...], preferred_element_type=F32)
    o_ref[...] = _layer_norm(DEEPNORM_ALPHA * h + MACARON_WEIGHT * ff, g_ref[...], b_ref[...])


def _ffn_ln(h, wg, wu, wd, g, b, tm=256):
    T, D = h.shape
    FF = wg.shape[1]
    return pl.pallas_call(
        _ffn_kernel,
        grid=(T // tm,),
        in_specs=[
            pl.BlockSpec((tm, D), lambda i: (i, 0)),
            _const_spec((D, FF)), _const_spec((D, FF)), _const_spec((FF, D)),
            _const_spec((1, D)), _const_spec((1, D)),
        ],
        out_specs=pl.BlockSpec((tm, D), lambda i: (i, 0)),
        out_shape=jax.ShapeDtypeStruct((T, D), F32),
        compiler_params=_params("parallel"),
        name="ffn_ln",
    )(h, wg, wu, wd, g.reshape(1, D), b.reshape(1, D))


def _proj_kernel(x_ref, w_ref, o_ref):
    o_ref[...] = jnp.dot(x_ref[...].astype(BF16), w_ref[...],
                         preferred_element_type=F32).astype(o_ref.dtype)


def _proj(x, w, out_dtype, tm=512):
    T, D = x.shape
    N = w.shape[1]
    return pl.pallas_call(
        _proj_kernel,
        grid=(T // tm,),
        in_specs=[pl.BlockSpec((tm, D), lambda i: (i, 0)), _const_spec((D, N))],
        out_specs=pl.BlockSpec((tm, N), lambda i: (i, 0)),
        out_shape=jax.ShapeDtypeStruct((T, N), out_dtype),
        compiler_params=_params("parallel"),
        name="proj",
    )(x, w)


def _outproj_kernel(y_ref, w_ref, h_ref, g_ref, b_ref, o_ref):
    mix = jnp.dot(y_ref[...].astype(BF16), w_ref[...], preferred_element_type=F32)
    o_ref[...] = _layer_norm(DEEPNORM_ALPHA * h_ref[...] + mix, g_ref[...], b_ref[...])


def _outproj_ln(y, w, h, g, b, tm=512):
    T, K = y.shape
    D = w.shape[1]
    return pl.pallas_call(
        _outproj_kernel,
        grid=(T // tm,),
        in_specs=[
            pl.BlockSpec((tm, K), lambda i: (i, 0)), _const_spec((K, D)),
            pl.BlockSpec((tm, D), lambda i: (i, 0)),
            _const_spec((1, D)), _const_spec((1, D)),
        ],
        out_specs=pl.BlockSpec((tm, D), lambda i: (i, 0)),
        out_shape=jax.ShapeDtypeStruct((T, D), F32),
        compiler_params=_params("parallel"),
        name="outproj_ln",
    )(y, w, h, g.reshape(1, D), b.reshape(1, D))


def _ssd_kernel(z_ref, xbc_ref, dt_ref, cw_ref, cb_ref, dtb_ref, alog_ref, dskip_ref, nw_ref,
                y_ref, pad_ref, state_ref, ydiag_ref):
    Q = SSM_CHUNK
    c = pl.program_id(1)

    @pl.when(c == 0)
    def _():
        pad_ref[0:8, :] = jnp.zeros((8, SSM_CONV_DIM), F32)
        state_ref[...] = jnp.zeros_like(state_ref)

    xraw = xbc_ref[0]
    pad_ref[8:8 + Q, :] = xraw
    conv = cb_ref[...] + cw_ref[SSM_CONV - 1:SSM_CONV, :] * xraw
    for k in range(1, SSM_CONV):
        conv = conv + cw_ref[SSM_CONV - 1 - k:SSM_CONV - k, :] * pad_ref[pl.ds(8 - k, Q), :]
    pad_ref[0:8, :] = xraw[Q - 8:Q, :]
    xbc = _silu(conv)
    xs = xbc[:, :SSM_D_INNER]

    dt = _softplus(dt_ref[0] + dtb_ref[...])
    adt = dt * (-jnp.exp(alog_ref[...]))
    row = lax.broadcasted_iota(jnp.int32, (Q, Q), 0)
    col = lax.broadcasted_iota(jnp.int32, (Q, Q), 1)
    tril = row >= col
    cs = _dot3_left(tril.astype(BF16), adt)
    cs_t = cs.T
    cs_last = cs[Q - 1:Q, :]

    eh = lax.broadcasted_iota(jnp.int32, (LANES, SSM_D_INNER), 0)
    ec = lax.broadcasted_iota(jnp.int32, (LANES, SSM_D_INNER), 1)
    expand = (ec // SSM_HEAD_DIM == eh).astype(BF16)
    xdt = xs * _dot3(dt, expand)
    dec_out = _dot3(jnp.exp(cs), expand)
    dec_end = _dot3(jnp.exp(cs_last - cs), expand)
    dec_chunk = _dot3(jnp.broadcast_to(jnp.exp(cs_last), (8, LANES)), expand)[0:1, :]

    xdt_b = xdt.astype(BF16)
    xend_b = (xdt * dec_end).astype(BF16)
    lane = lax.broadcasted_iota(jnp.int32, (Q, LANES), 1)
    low_half = lane < SSM_HEAD_DIM

    for g in range(SSM_GROUPS):
        bg = xbc[:, SSM_D_INNER + g * SSM_D_STATE:SSM_D_INNER + (g + 1) * SSM_D_STATE]
        cg = xbc[:, SSM_D_INNER + SSM_BC_DIM + g * SSM_D_STATE:
                 SSM_D_INNER + SSM_BC_DIM + (g + 1) * SSM_D_STATE]
        bg_b = bg.astype(BF16)
        cg_b = cg.astype(BF16)
        cb = lax.dot_general(cg_b, bg_b, (((1,), (1,)), ((), ())), preferred_element_type=F32)
        for pr in range(SSM_HEADS_PER_GROUP // 2):
            pair = g * (SSM_HEADS_PER_GROUP // 2) + pr
            xp = xdt_b[:, pair * LANES:(pair + 1) * LANES]
            acc = None
            for e in range(2):
                h = 2 * pair + e
                seg = cs[:, h:h + 1] - cs_t[h:h + 1, :]
                m = (cb * jnp.exp(jnp.where(tril, seg, NEG_INF))).astype(BF16)
                xm = jnp.where(low_half if e == 0 else jnp.logical_not(low_half), xp, jnp.zeros_like(xp))
                t = jnp.dot(m, xm, preferred_element_type=F32)
                acc = t if acc is None else acc + t
            ydiag_ref[:, pair * LANES:(pair + 1) * LANES] = acc
        gs = slice(g * SSM_GROUP_WIDTH, (g + 1) * SSM_GROUP_WIDTH)
        st = state_ref[g]
        y_off = jnp.dot(cg_b, st.astype(BF16), preferred_element_type=F32) * dec_out[:, gs]
        ydiag_ref[:, gs] = ydiag_ref[:, gs] + y_off
        upd = jnp.dot(bg.T.astype(BF16), xend_b[:, gs], preferred_element_type=F32)
        state_ref[g] = st * dec_chunk[:, gs] + upd

    y = ydiag_ref[...] + xs * dskip_ref[...]
    y = y * _silu(z_ref[0])
    for g in range(SSM_GROUPS):
        gs = slice(g * SSM_GROUP_WIDTH, (g + 1) * SSM_GROUP_WIDTH)
        yg = y[:, gs]
        yg = yg * lax.rsqrt(jnp.mean(yg * yg, -1, keepdims=True) + LN_EPS)
        y_ref[0, :, gs] = (yg * nw_ref[:, gs]).astype(y_ref.dtype)


def _ssd_core(z, xbc, dt, conv_w, conv_b, dt_bias, a_log, d_skip, norm_w):
    B, S, _ = z.shape
    Q = SSM_CHUNK
    nc = S // Q
    blk = lambda w: pl.BlockSpec((1, Q, w), lambda b, c: (b, c, 0))
    return pl.pallas_call(
        _ssd_kernel,
        grid=(B, nc),
        in_specs=[
            blk(SSM_D_INNER), blk(SSM_CONV_DIM), blk(LANES),
            _const_spec((SSM_CONV, SSM_CONV_DIM)), _const_spec((1, SSM_CONV_DIM)),
            _const_spec((1, LANES)), _const_spec((1, LANES)),
            _const_spec((1, SSM_D_INNER)), _const_spec((1, SSM_D_INNER)),
        ],
        out_specs=blk(SSM_D_INNER),
        out_shape=jax.ShapeDtypeStruct((B, S, SSM_D_INNER), BF16),
        scratch_shapes=[
            pltpu.VMEM((8 + Q, SSM_CONV_DIM), F32),
            pltpu.VMEM((SSM_GROUPS, SSM_D_STATE, SSM_GROUP_WIDTH), F32),
            pltpu.VMEM((Q, SSM_D_INNER), F32),
        ],
        compiler_params=_params("arbitrary", "arbitrary"),
        name="ssd_core",
    )(z, xbc, dt, conv_w, conv_b.reshape(1, -1), _pad_lanes(dt_bias.reshape(1, -1)),
      _pad_lanes(a_log.reshape(1, -1)),
      jnp.repeat(d_skip, SSM_HEAD_DIM).reshape(1, -1), norm_w.reshape(1, -1))


def _mamba2_mixer(h, B, S, w_in, conv_w, conv_b, dt_bias, a_log, d_skip, norm_w, w_out, ln_g, ln_b):
    wz = w_in[:, :SSM_D_INNER].astype(BF16)
    wx = w_in[:, SSM_D_INNER:SSM_D_INNER + SSM_CONV_DIM].astype(BF16)
    wdt = _pad_lanes(w_in[:, SSM_D_INNER + SSM_CONV_DIM:]).astype(BF16)
    z = _proj(h, wz, F32).reshape(B, S, -1)
    xbc = _proj(h, wx, F32).reshape(B, S, -1)
    dt = _proj(h, wdt, F32).reshape(B, S, -1)
    y = _ssd_core(z, xbc, dt, conv_w, conv_b, dt_bias, a_log, d_skip, norm_w)
    return _outproj_ln(y.reshape(B * S, -1), w_out.astype(BF16), h, ln_g, ln_b)


def _head_masks(n_rows, width):
    lane = lax.broadcasted_iota(jnp.int32, (n_rows, width), 1)
    low = (lane & ATT_HEAD_DIM) == 0
    return [low, jnp.logical_not(low)]


def _softmax_tile(s, vt, m_ref, l_ref, acc_ref, first=False):
    s_max = jnp.max(s, 0, keepdims=True)
    if first:
        m_new = s_max
        p = jnp.exp(s - m_new)
        l_ref[...] = jnp.sum(p, 0, keepdims=True)
        acc_ref[...] = jnp.dot(vt, p.astype(BF16), preferred_element_type=F32)
    else:
        m_old = m_ref[...]
        m_new = jnp.maximum(m_old, s_max)
        a = jnp.exp(m_old - m_new)
        p = jnp.exp(s - m_new)
        l_ref[...] = a * l_ref[...] + jnp.sum(p, 0, keepdims=True)
        acc_ref[...] = a * acc_ref[...] + jnp.dot(vt, p.astype(BF16), preferred_element_type=F32)
    m_ref[...] = m_new


def _finish_pair(o_ref, l_ref, acc_ref, tq):
    d = ATT_HEAD_DIM
    ot = jnp.concatenate([acc_ref[0:d, 0:tq] / l_ref[:, 0:tq],
                          acc_ref[d:2 * d, tq:2 * tq] / l_ref[:, tq:2 * tq]], axis=0)
    o_ref[0] = ot.T.astype(o_ref.dtype)


def _fox_proj_kernel(x_ref, wq_ref, wk_ref, wvt_ref, wf_ref, q_ref, k_ref, vt_ref, f_ref):
    xb = x_ref[...].astype(BF16)
    q = jnp.dot(xb, wq_ref[...], preferred_element_type=F32)
    q_ref[...] = (q * ATT_SCALE).astype(BF16)
    k_ref[...] = jnp.dot(xb, wk_ref[...], preferred_element_type=F32).astype(BF16)
    vt_ref[0] = lax.dot_general(wvt_ref[...], xb, (((1,), (1,)), ((), ())),
                                preferred_element_type=F32).astype(BF16)
    f_ref[...] = jnp.dot(xb, wf_ref[...], preferred_element_type=F32)


def _fox_proj(h, B, S, wq, wk, wvt, wf, tm=512):
    T, D = h.shape
    HD = wq.shape[1]
    spt = S // tm
    return pl.pallas_call(
        _fox_proj_kernel,
        grid=(T // tm,),
        in_specs=[pl.BlockSpec((tm, D), lambda i: (i, 0)),
                  _const_spec((D, HD)), _const_spec((D, HD)), _const_spec((HD, D)),
                  _const_spec((D, LANES))],
        out_specs=[pl.BlockSpec((tm, HD), lambda i: (i, 0)),
                   pl.BlockSpec((tm, HD), lambda i: (i, 0)),
                   pl.BlockSpec((1, HD, tm), lambda i: (i // spt, 0, i % spt)),
                   pl.BlockSpec((tm, LANES), lambda i: (i, 0))],
        out_shape=[jax.ShapeDtypeStruct((T, HD), BF16), jax.ShapeDtypeStruct((T, HD), BF16),
                   jax.ShapeDtypeStruct((B, HD, S), BF16), jax.ShapeDtypeStruct((T, LANES), F32)],
        compiler_params=_params("parallel"),
        name="fox_proj",
    )(h, wq, wk, wvt, wf)


FOX_AUG_K = 0
FOX_AUG_Q = 3


def _cum_kernel(f_ref, bf_ref, qa_ref, ka_ref, carry_ref):
    tc = f_ref.shape[1]
    W = qa_ref.shape[2]

    @pl.when(pl.program_id(1) == 0)
    def _():
        carry_ref[...] = jnp.zeros_like(carry_ref)

    x = f_ref[0] + bf_ref[...]
    log_f = -_softplus(-x)
    row = lax.broadcasted_iota(jnp.int32, (tc, tc), 0)
    col = lax.broadcasted_iota(jnp.int32, (tc, tc), 1)
    cum = _dot3_left((row >= col).astype(BF16), log_f) + carry_ref[0:1, :]
    carry_ref[...] = jnp.broadcast_to(cum[tc - 1:tc, :], carry_ref.shape)

    hh = lax.broadcasted_iota(jnp.int32, (LANES, W), 0)
    cc = lax.broadcasted_iota(jnp.int32, (LANES, W), 1)
    base = (hh >> 1) * LANES + (hh & 1) * ATT_HEAD_DIM
    valid = hh < ATT_HEADS
    within = lax.broadcasted_iota(jnp.int32, (tc, W), 1) & (ATT_HEAD_DIM - 1)
    parts = _split3(cum)
    qa = jnp.where(jnp.logical_and(within >= FOX_AUG_K, within < FOX_AUG_K + 3), -1.0, 0.0)
    ka = jnp.where(jnp.logical_and(within >= FOX_AUG_Q, within < FOX_AUG_Q + 3), 1.0, 0.0)
    for c, part in enumerate(parts):
        pk = jnp.logical_and(valid, cc == base + (FOX_AUG_K + c)).astype(BF16)
        pq = jnp.logical_and(valid, cc == base + (FOX_AUG_Q + c)).astype(BF16)
        ka = ka + jnp.dot(part, pk, preferred_element_type=F32)
        qa = qa + jnp.dot(part, pq, preferred_element_type=F32)
    qa_ref[0] = qa.astype(BF16)
    ka_ref[0] = ka.astype(BF16)


def _fox_cum(f, b_f, width, tc=256):
    B, S, W = f.shape
    return pl.pallas_call(
        _cum_kernel,
        grid=(B, S // tc),
        in_specs=[pl.BlockSpec((1, tc, W), lambda b, c: (b, c, 0)), _const_spec((1, W))],
        out_specs=[pl.BlockSpec((1, tc, width), lambda b, c: (b, c, 0)),
                   pl.BlockSpec((1, tc, width), lambda b, c: (b, c, 0))],
        out_shape=[jax.ShapeDtypeStruct((B, S, width), BF16), jax.ShapeDtypeStruct((B, S, width), BF16)],
        scratch_shapes=[pltpu.VMEM((8, W), F32)],
        compiler_params=_params("arbitrary", "arbitrary"),
        name="fox_cum",
    )(f, _pad_lanes(b_f.reshape(1, -1), W))


def _fox_kernel(q_ref, qa_ref, k_ref, ka_ref, vt_ref, o_ref, m_ref, l_ref, acc_ref, *, t):
    qi = pl.program_id(2)
    q2 = jnp.concatenate([q_ref[0], qa_ref[0]], axis=1)
    masks = _head_masks(t, 2 * LANES)
    qcat = jnp.concatenate([jnp.where(masks[e], q2, jnp.zeros_like(q2)) for e in range(2)], axis=0)
    kpos = lax.broadcasted_iota(jnp.int32, (t, 2 * t), 0)
    qpos = lax.broadcasted_iota(jnp.int32, (t, 2 * t), 1) & (t - 1)
    causal = kpos <= qpos

    def kv_step(j, diag):
        ks = pl.multiple_of(j * t, t)
        k2 = jnp.concatenate([k_ref[0, pl.ds(ks, t), :], ka_ref[0, pl.ds(ks, t), :]], axis=1)
        s = lax.dot_general(k2, qcat, (((1,), (1,)), ((), ())), preferred_element_type=F32)
        if diag:
            s = jnp.where(causal, s, NEG_INF)
        _softmax_tile(s, vt_ref[0, :, pl.ds(ks, t)], m_ref, l_ref, acc_ref, first=diag)

    kv_step(qi, True)

    def body(j, carry):
        kv_step(j, False)
        return carry

    lax.fori_loop(0, qi, body, 0)
    _finish_pair(o_ref, l_ref, acc_ref, t)


def _fox_attention_core(q, qa, k, ka, vt, t=512):
    B, S, D = q.shape
    kern = functools.partial(_fox_kernel, t=t)
    tile = pl.BlockSpec((1, t, LANES), lambda b, hp, i: (b, i, hp))
    full = pl.BlockSpec((1, S, LANES), lambda b, hp, i: (b, 0, hp))
    return pl.pallas_call(
        kern,
        grid=(B, D // LANES, S // t),
        in_specs=[tile, tile, full, full,
                  pl.BlockSpec((1, LANES, S), lambda b, hp, i: (b, hp, 0))],
        out_specs=tile,
        out_shape=jax.ShapeDtypeStruct((B, S, D), BF16),
        scratch_shapes=[
            pltpu.VMEM((1, 2 * t), F32), pltpu.VMEM((1, 2 * t), F32),
            pltpu.VMEM((LANES, 2 * t), F32),
        ],
        compiler_params=_params("parallel", "parallel", "arbitrary"),
        name="fox_attention",
    )(q, qa, k, ka, vt)


def _fox_mixer(h, B, S, w_in, b_f, w_out, ln_g, ln_b):
    HD = ATT_HEADS * ATT_HEAD_DIM
    wq = w_in[:, :HD].astype(BF16)
    wk = w_in[:, HD:2 * HD].astype(BF16)
    wvt = w_in[:, 2 * HD:3 * HD].T.astype(BF16)
    wf = _pad_lanes(w_in[:, 3 * HD:]).astype(BF16)
    q, k, vt, f = _fox_proj(h, B, S, wq, wk, wvt, wf)
    qa, ka = _fox_cum(f.reshape(B, S, LANES), b_f, HD)
    o = _fox_attention_core(q.reshape(B, S, HD), qa, k.reshape(B, S, HD), ka, vt)
    return _outproj_ln(o.reshape(B * S, HD), w_out.astype(BF16), h, ln_g, ln_b)


def _moba_proj_kernel(x_ref, wq_ref, wk_ref, wvt_ref, q_ref, k_ref, vt_ref):
    xb = x_ref[...].astype(BF16)
    q_ref[...] = jnp.dot(xb, wq_ref[...], preferred_element_type=F32)
    k_ref[...] = jnp.dot(xb, wk_ref[...], preferred_element_type=F32).astype(BF16)
    vt_ref[0] = lax.dot_general(wvt_ref[...], xb, (((1,), (1,)), ((), ())),
                                preferred_element_type=F32).astype(BF16)


def _moba_proj(h, B, S, wq, wk, wvt, tm=512):
    T, D = h.shape
    HD = wq.shape[1]
    spt = S // tm
    return pl.pallas_call(
        _moba_proj_kernel,
        grid=(T // tm,),
        in_specs=[pl.BlockSpec((tm, D), lambda i: (i, 0)),
                  _const_spec((D, HD)), _const_spec((D, HD)), _const_spec((HD, D))],
        out_specs=[pl.BlockSpec((tm, HD), lambda i: (i, 0)),
                   pl.BlockSpec((tm, HD), lambda i: (i, 0)),
                   pl.BlockSpec((1, HD, tm), lambda i: (i // spt, 0, i % spt))],
        out_shape=[jax.ShapeDtypeStruct((T, HD), F32), jax.ShapeDtypeStruct((T, HD), BF16),
                   jax.ShapeDtypeStruct((B, HD, S), BF16)],
        compiler_params=_params("parallel"),
        name="moba_proj",
    )(h, wq, wk, wvt)


def _moba_kernel(qf_ref, k_ref, vt_ref, o_ref, m_ref, l_ref, acc_ref, km_ref, bias_ref, *, nb):
    L = MOBA_BLOCK
    qt = pl.program_id(2)

    @pl.when(qt == 0)
    def _():
        for n in range(nb):
            kb = k_ref[0, n * L:(n + 1) * L, :].astype(F32)
            km_ref[n:n + 1, :] = jnp.sum(kb, 0, keepdims=True) * (1.0 / L)

    t = 2 * L
    masks = _head_masks(t, LANES)
    qf = qf_ref[0]
    qfcat = jnp.concatenate([jnp.where(masks[e], qf, jnp.zeros_like(qf)) for e in range(2)], axis=0)
    qcat = (qfcat * ATT_SCALE).astype(BF16)
    colq = lax.broadcasted_iota(jnp.int32, (1, 2 * t), 1) & (t - 1)
    second = colq >= L
    own = (2 * qt + second.astype(jnp.int32)).astype(F32)
    blk = lax.broadcasted_iota(jnp.int32, (nb, 2 * t), 0).astype(F32)

    gate = lax.dot_general(km_ref[...], qfcat, (((1,), (1,)), ((), ())), preferred_element_type=F32,
                           precision=lax.Precision.HIGHEST)
    gate = jnp.where(blk < own, gate, NEG_INF)
    bias = jnp.full((nb, 2 * t), NEG_INF, F32)
    for _ in range(MOBA_TOPK):
        mx = jnp.max(gate, 0, keepdims=True)
        first = jnp.min(jnp.where(gate == mx, blk, float(nb)), 0, keepdims=True)
        pick = jnp.logical_and(blk == first, mx > NEG_INF)
        bias = jnp.where(pick, 0.0, bias)
        gate = jnp.where(pick, NEG_INF, gate)
    bias_ref[...] = bias

    def block_bias(n):
        return jnp.broadcast_to(bias_ref[pl.ds(n, 1), :], (L, 2 * t))

    ks = pl.multiple_of(qt * t, t)
    s = lax.dot_general(k_ref[0, pl.ds(ks, t), :], qcat, (((1,), (1,)), ((), ())),
                        preferred_element_type=F32)
    kpos = lax.broadcasted_iota(jnp.int32, (t, 2 * t), 0)
    qpos = lax.broadcasted_iota(jnp.int32, (t, 2 * t), 1) & (t - 1)
    s = jnp.where(kpos <= qpos, s, NEG_INF)
    past = jnp.where(second, bias_ref[pl.ds(2 * qt, 1), :], 0.0)
    s = jnp.concatenate([s[0:L] + past, s[L:t]], axis=0)
    _softmax_tile(s, vt_ref[0, :, pl.ds(ks, t)], m_ref, l_ref, acc_ref, first=True)

    def body(j, carry):
        ks = pl.multiple_of(j * t, t)
        s = lax.dot_general(k_ref[0, pl.ds(ks, t), :], qcat, (((1,), (1,)), ((), ())),
                            preferred_element_type=F32)
        s = s + jnp.concatenate([block_bias(2 * j), block_bias(2 * j + 1)], axis=0)
        _softmax_tile(s, vt_ref[0, :, pl.ds(ks, t)], m_ref, l_ref, acc_ref)
        return carry

    lax.fori_loop(0, qt, body, 0)
    _finish_pair(o_ref, l_ref, acc_ref, t)


def _moba_attention_core(qf, k, vt):
    B, S, D = qf.shape
    t = 2 * MOBA_BLOCK
    nb = S // MOBA_BLOCK
    tile = pl.BlockSpec((1, t, LANES), lambda b, hp, i: (b, i, hp))
    return pl.pallas_call(
        functools.partial(_moba_kernel, nb=nb),
        grid=(B, D // LANES, S // t),
        in_specs=[tile,
                  pl.BlockSpec((1, S, LANES), lambda b, hp, i: (b, 0, hp)),
                  pl.BlockSpec((1, LANES, S), lambda b, hp, i: (b, hp, 0))],
        out_specs=tile,
        out_shape=jax.ShapeDtypeStruct((B, S, D), BF16),
        scratch_shapes=[
            pltpu.VMEM((1, 2 * t), F32), pltpu.VMEM((1, 2 * t), F32),
            pltpu.VMEM((LANES, 2 * t), F32),
            pltpu.VMEM((nb, LANES), F32), pltpu.VMEM((nb, 2 * t), F32),
        ],
        compiler_params=_params("parallel", "arbitrary", "arbitrary"),
        name="moba_attention",
    )(qf, k, vt)


def _moba_mixer(h, B, S, w_in, w_out, ln_g, ln_b):
    HD = ATT_HEADS * ATT_HEAD_DIM
    wq = w_in[:, :HD].astype(BF16)
    wk = w_in[:, HD:2 * HD].astype(BF16)
    wvt = w_in[:, 2 * HD:3 * HD].T.astype(BF16)
    qf, k, vt = _moba_proj(h, B, S, wq, wk, wvt)
    o = _moba_attention_core(qf.reshape(B, S, HD), k.reshape(B, S, HD), vt)
    return _outproj_ln(o.reshape(B * S, HD), w_out.astype(BF16), h, ln_g, ln_b)


def kernel(x, ffn_w_gate, ffn_w_up, ffn_w_down, ln_g, ln_b, ssm_w_in, ssm_conv_w, ssm_conv_b,
           ssm_dt_bias, ssm_a_log, ssm_d, ssm_norm_w, ssm_w_out, fox_w_in, fox_b_f, fox_w_out,
           moba_w_in, moba_w_out):
    B, S, D = x.shape
    h = x.reshape(B * S, D)
    for layer in range(DEPTH):
        kind, j = layer % 3, layer // 3
        h = _ffn_ln(h, ffn_w_gate[layer, 0].astype(BF16), ffn_w_up[layer, 0].astype(BF16),
                    ffn_w_down[layer, 0].astype(BF16), ln_g[layer, 0], ln_b[layer, 0])
        if kind == 0:
            h = _mamba2_mixer(h, B, S, ssm_w_in[j], ssm_conv_w[j], ssm_conv_b[j], ssm_dt_bias[j],
                              ssm_a_log[j], ssm_d[j], ssm_norm_w[j], ssm_w_out[j],
                              ln_g[layer, 1], ln_b[layer, 1])
        elif kind == 1:
            h = _fox_mixer(h, B, S, fox_w_in[j], fox_b_f[j], fox_w_out[j],
                           ln_g[layer, 1], ln_b[layer, 1])
        else:
            h = _moba_mixer(h, B, S, moba_w_in[j], moba_w_out[j], ln_g[layer, 1], ln_b[layer, 1])
        h = _ffn_ln(h, ffn_w_gate[layer, 1].astype(BF16), ffn_w_up[layer, 1].astype(BF16),
                    ffn_w_down[layer, 1].astype(BF16), ln_g[layer, 2], ln_b[layer, 2])
    return h.reshape(B, S, D)
```

```python
import functools

import jax
import jax.numpy as jnp
from jax import lax
from jax.experimental import pallas as pl
from jax.experimental.pallas import tpu as pltpu

F32 = jnp.float32
BF16 = jnp.bfloat16

DEPTH = 4
D_MODEL = 1024
D_FF = 2816
LN_EPS = 1e-5
DEEPNORM_ALPHA = (2 * DEPTH) ** 0.25
MACARON_WEIGHT = 0.5

SSM_D_INNER = 2048
SSM_HEAD_DIM = 64
SSM_HEADS = 32
SSM_GROUPS = 4
SSM_HEADS_PER_GROUP = 8
SSM_D_STATE = 128
SSM_CONV = 4
SSM_CHUNK = 256
SSM_BC_DIM = SSM_GROUPS * SSM_D_STATE
SSM_CONV_DIM = SSM_D_INNER + 2 * SSM_BC_DIM
SSM_GROUP_WIDTH = SSM_D_INNER // SSM_GROUPS

ATT_HEAD_DIM = 64
ATT_HEADS = 16
ATT_SCALE = ATT_HEAD_DIM ** -0.5
MOBA_BLOCK = 256
MOBA_TOPK = 3

LANES = 128
NEG_INF = float("-inf")
VMEM_LIMIT = 56 * 1024 * 1024


def _params(*sem):
    return pltpu.CompilerParams(dimension_semantics=sem, vmem_limit_bytes=VMEM_LIMIT)


def _const_spec(shape):
    nd = len(shape)
    return pl.BlockSpec(shape, lambda *_: (0,) * nd, pipeline_mode=pl.Buffered(1))


def _layer_norm(v, g, b):
    mu = jnp.mean(v, -1, keepdims=True)
    d = v - mu
    var = jnp.mean(d * d, -1, keepdims=True)
    return d * lax.rsqrt(var + LN_EPS) * g + b


def _split3(v):
    hi = v.astype(BF16)
    r1 = v - hi.astype(F32)
    mid = r1.astype(BF16)
    lo = (r1 - mid.astype(F32)).astype(BF16)
    return hi, mid, lo


def _dot3(v, m01):
    out = None
    for part in _split3(v):
        t = jnp.dot(part, m01, preferred_element_type=F32)
        out = t if out is None else out + t
    return out


def _dot3_left(m01, v):
    out = None
    for part in _split3(v):
        t = jnp.dot(m01, part, preferred_element_type=F32)
        out = t if out is None else out + t
    return out


def _softplus(x):
    return jnp.maximum(x, 0.0) + jnp.log1p(jnp.exp(-jnp.abs(x)))


def _silu(x):
    return x * jax.nn.sigmoid(x)


def _pad_lanes(v, n=LANES):
    return jnp.pad(v, [(0, 0)] * (v.ndim - 1) + [(0, n - v.shape[-1])])


def _ffn_kernel(h_ref, wg_ref, wu_ref, wd_ref, g_ref, b_ref, o_ref):
    h = h_ref[...]
    hb = h.astype(BF16)
    gate = jnp.dot(hb, wg_ref[...], preferred_element_type=F32)
    up = jnp.dot(hb, wu_ref[...], preferred_element_type=F32)
    act = (_silu(gate) * up).astype(BF16)
    ff = jnp.dot(act, wd_ref[...], preferred_element_type=F32)
    o_ref[...] = _layer_norm(DEEPNORM_ALPHA * h + MACARON_WEIGHT * ff, g_ref[...], b_ref[...])


def _ffn_ln(h, wg, wu, wd, g, b, tm=512):
    T, D = h.shape
    FF = wg.shape[1]
    return pl.pallas_call(
        _ffn_kernel,
        grid=(T // tm,),
        in_specs=[
            pl.BlockSpec((tm, D), lambda i: (i, 0)),
            _const_spec((D, FF)), _const_spec((D, FF)), _const_spec((FF, D)),
            _const_spec((1, D)), _const_spec((1, D)),
        ],
        out_specs=pl.BlockSpec((tm, D), lambda i: (i, 0)),
        out_shape=jax.ShapeDtypeStruct((T, D), F32),
        compiler_params=_params("parallel"),
        name="ffn_ln",
    )(h, wg, wu, wd, g.reshape(1, D), b.reshape(1, D))


def _proj_kernel(x_ref, w_ref, o_ref):
    o_ref[...] = jnp.dot(x_ref[...].astype(BF16), w_ref[...],
                         preferred_element_type=F32).astype(o_ref.dtype)


def _proj(x, w, out_dtype, tm=512):
    T, D = x.shape
    N = w.shape[1]
    return pl.pallas_call(
        _proj_kernel,
        grid=(T // tm,),
        in_specs=[pl.BlockSpec((tm, D), lambda i: (i, 0)), _const_spec((D, N))],
        out_specs=pl.BlockSpec((tm, N), lambda i: (i, 0)),
        out_shape=jax.ShapeDtypeStruct((T, N), out_dtype),
        compiler_params=_params("parallel"),
        name="proj",
    )(x, w)


def _outproj_kernel(y_ref, w_ref, h_ref, g_ref, b_ref, o_ref):
    mix = jnp.dot(y_ref[...].astype(BF16), w_ref[...], preferred_element_type=F32)
    o_ref[...] = _layer_norm(DEEPNORM_ALPHA * h_ref[...] + mix, g_ref[...], b_ref[...])


def _outproj_ln(y, w, h, g, b, tm=512):
    T, K = y.shape
    D = w.shape[1]
    return pl.pallas_call(
        _outproj_kernel,
        grid=(T // tm,),
        in_specs=[
            pl.BlockSpec((tm, K), lambda i: (i, 0)), _const_spec((K, D)),
            pl.BlockSpec((tm, D), lambda i: (i, 0)),
            _const_spec((1, D)), _const_spec((1, D)),
        ],
        out_specs=pl.BlockSpec((tm, D), lambda i: (i, 0)),
        out_shape=jax.ShapeDtypeStruct((T, D), F32),
        compiler_params=_params("parallel"),
        name="outproj_ln",
    )(y, w, h, g.reshape(1, D), b.reshape(1, D))


def _ssd_kernel(z_ref, xbc_ref, dt_ref, cw_ref, cb_ref, dtb_ref, alog_ref, dskip_ref, nw_ref,
                y_ref, pad_ref, state_ref, ydiag_ref):
    Q = SSM_CHUNK
    c = pl.program_id(1)

    @pl.when(c == 0)
    def _():
        pad_ref[0:8, :] = jnp.zeros((8, SSM_CONV_DIM), F32)
        state_ref[...] = jnp.zeros_like(state_ref)

    xraw = xbc_ref[0]
    pad_ref[8:8 + Q, :] = xraw
    conv = cb_ref[...] + cw_ref[SSM_CONV - 1:SSM_CONV, :] * xraw
    for k in range(1, SSM_CONV):
        conv = conv + cw_ref[SSM_CONV - 1 - k:SSM_CONV - k, :] * pad_ref[pl.ds(8 - k, Q), :]
    pad_ref[0:8, :] = xraw[Q - 8:Q, :]
    xbc = _silu(conv)
    xs = xbc[:, :SSM_D_INNER]

    dt = _softplus(dt_ref[0] + dtb_ref[...])
    adt = dt * (-jnp.exp(alog_ref[...]))
    row = lax.broadcasted_iota(jnp.int32, (Q, Q), 0)
    col = lax.broadcasted_iota(jnp.int32, (Q, Q), 1)
    tril = row >= col
    cs = _dot3_left(tril.astype(BF16), adt)
    cs_t = cs.T
    cs_last = cs[Q - 1:Q, :]

    eh = lax.broadcasted_iota(jnp.int32, (LANES, SSM_D_INNER), 0)
    ec = lax.broadcasted_iota(jnp.int32, (LANES, SSM_D_INNER), 1)
    expand = (ec // SSM_HEAD_DIM == eh).astype(BF16)
    xdt = xs * _dot3(dt, expand)
    dec_out = _dot3(jnp.exp(cs), expand)
    dec_end = _dot3(jnp.exp(cs_last - cs), expand)
    dec_chunk = _dot3(jnp.broadcast_to(jnp.exp(cs_last), (8, LANES)), expand)[0:1, :]

    xdt_b = xdt.astype(BF16)
    xend_b = (xdt * dec_end).astype(BF16)
    lane = lax.broadcasted_iota(jnp.int32, (Q, LANES), 1)
    low_half = lane < SSM_HEAD_DIM

    for g in range(SSM_GROUPS):
        bg = xbc[:, SSM_D_INNER + g * SSM_D_STATE:SSM_D_INNER + (g + 1) * SSM_D_STATE]
        cg = xbc[:, SSM_D_INNER + SSM_BC_DIM + g * SSM_D_STATE:
                 SSM_D_INNER + SSM_BC_DIM + (g + 1) * SSM_D_STATE]
        bg_b = bg.astype(BF16)
        cg_b = cg.astype(BF16)
        cb = lax.dot_general(cg_b, bg_b, (((1,), (1,)), ((), ())), preferred_element_type=F32)
        for pr in range(SSM_HEADS_PER_GROUP // 2):
            pair = g * (SSM_HEADS_PER_GROUP // 2) + pr
            xp = xdt_b[:, pair * LANES:(pair + 1) * LANES]
            acc = None
            for e in range(2):
                h = 2 * pair + e
                seg = cs[:, h:h + 1] - cs_t[h:h + 1, :]
                m = (cb * jnp.exp(jnp.where(tril, seg, NEG_INF))).astype(BF16)
                xm = jnp.where(low_half if e == 0 else jnp.logical_not(low_half), xp, jnp.zeros_like(xp))
                t = jnp.dot(m, xm, preferred_element_type=F32)
                acc = t if acc is None else acc + t
            ydiag_ref[:, pair * LANES:(pair + 1) * LANES] = acc
        gs = slice(g * SSM_GROUP_WIDTH, (g + 1) * SSM_GROUP_WIDTH)
        st = state_ref[g]
        y_off = jnp.dot(cg_b, st.astype(BF16), preferred_element_type=F32) * dec_out[:, gs]
        ydiag_ref[:, gs] = ydiag_ref[:, gs] + y_off
        upd = jnp.dot(bg.T.astype(BF16), xend_b[:, gs], preferred_element_type=F32)
        state_ref[g] = st * dec_chunk[:, gs] + upd

    y = ydiag_ref[...] + xs * dskip_ref[...]
    y = y * _silu(z_ref[0])
    for g in range(SSM_GROUPS):
        gs = slice(g * SSM_GROUP_WIDTH, (g + 1) * SSM_GROUP_WIDTH)
        yg = y[:, gs]
        yg = yg * lax.rsqrt(jnp.mean(yg * yg, -1, keepdims=True) + LN_EPS)
        y_ref[0, :, gs] = (yg * nw_ref[:, gs]).astype(y_ref.dtype)


def _ssd_core(z, xbc, dt, conv_w, conv_b, dt_bias, a_log, d_skip, norm_w):
    B, S, _ = z.shape
    Q = SSM_CHUNK
    nc = S // Q
    blk = lambda w: pl.BlockSpec((1, Q, w), lambda b, c: (b, c, 0))
    return pl.pallas_call(
        _ssd_kernel,
        grid=(B, nc),
        in_specs=[
            blk(SSM_D_INNER), blk(SSM_CONV_DIM), blk(LANES),
            _const_spec((SSM_CONV, SSM_CONV_DIM)), _const_spec((1, SSM_CONV_DIM)),
            _const_spec((1, LANES)), _const_spec((1, LANES)),
            _const_spec((1, SSM_D_INNER)), _const_spec((1, SSM_D_INNER)),
        ],
        out_specs=blk(SSM_D_INNER),
        out_shape=jax.ShapeDtypeStruct((B, S, SSM_D_INNER), BF16),
        scratch_shapes=[
            pltpu.VMEM((8 + Q, SSM_CONV_DIM), F32),
            pltpu.VMEM((SSM_GROUPS, SSM_D_STATE, SSM_GROUP_WIDTH), F32),
            pltpu.VMEM((Q, SSM_D_INNER), F32),
        ],
        compiler_params=_params("arbitrary", "arbitrary"),
        name="ssd_core",
    )(z, xbc, dt, conv_w, conv_b.reshape(1, -1), _pad_lanes(dt_bias.reshape(1, -1)),
      _pad_lanes(a_log.reshape(1, -1)),
      jnp.repeat(d_skip, SSM_HEAD_DIM).reshape(1, -1), norm_w.reshape(1, -1))


def _mamba2_mixer(h, B, S, w_in, conv_w, conv_b, dt_bias, a_log, d_skip, norm_w, w_out, ln_g, ln_b):
    wz = w_in[:, :SSM_D_INNER].astype(BF16)
    wx = w_in[:, SSM_D_INNER:SSM_D_INNER + SSM_CONV_DIM].astype(BF16)
    wdt = _pad_lanes(w_in[:, SSM_D_INNER + SSM_CONV_DIM:]).astype(BF16)
    z = _proj(h, wz, F32).reshape(B, S, -1)
    xbc = _proj(h, wx, F32).reshape(B, S, -1)
    dt = _proj(h, wdt, F32).reshape(B, S, -1)
    y = _ssd_core(z, xbc, dt, conv_w, conv_b, dt_bias, a_log, d_skip, norm_w)
    return _outproj_ln(y.reshape(B * S, -1), w_out.astype(BF16), h, ln_g, ln_b)


def _head_masks(n_rows, width):
    lane = lax.broadcasted_iota(jnp.int32, (n_rows, width), 1)
    low = (lane & ATT_HEAD_DIM) == 0
    return [low, jnp.logical_not(low)]


ATT_GROUP = 512


def _attn_step(k2, qcat, vt, adjust, m_ref, l_ref, acc_ref, tq, first=False):
    d, g = ATT_HEAD_DIM, ATT_GROUP
    n_groups = 2 * tq // g

    def qk(c):
        return lax.dot_general(k2, qcat[c * g:(c + 1) * g], (((1,), (1,)), ((), ())),
                               preferred_element_type=F32)

    s_next = qk(0)
    for c in range(n_groups):
        s = s_next
        if c + 1 < n_groups:
            s_next = qk(c + 1)
        s = adjust(c, s)
        cols = slice(c * g, (c + 1) * g)
        rows = slice((c * g // tq) * d, (c * g // tq + 1) * d)
        s_max = jnp.max(s, 0, keepdims=True)
        if first:
            m_new = s_max
            p = jnp.exp(s - m_new)
            l_ref[:, cols] = jnp.sum(p, 0, keepdims=True)
            acc_ref[rows, cols] = jnp.dot(vt[rows], p.astype(BF16), preferred_element_type=F32)
        else:
            m_old = m_ref[:, cols]
            m_new = jnp.maximum(m_old, s_max)
            a = jnp.exp(m_old - m_new)
            p = jnp.exp(s - m_new)
            l_ref[:, cols] = a * l_ref[:, cols] + jnp.sum(p, 0, keepdims=True)
            acc_ref[rows, cols] = a * acc_ref[rows, cols] + jnp.dot(vt[rows], p.astype(BF16),
                                                                     preferred_element_type=F32)
        m_ref[:, cols] = m_new


def _finish_pair(o_ref, l_ref, acc_ref, tq):
    d = ATT_HEAD_DIM
    ot = jnp.concatenate([acc_ref[0:d, 0:tq] / l_ref[:, 0:tq],
                          acc_ref[d:2 * d, tq:2 * tq] / l_ref[:, tq:2 * tq]], axis=0)
    o_ref[0] = ot.T.astype(o_ref.dtype)


def _fox_proj_kernel(x_ref, wq_ref, wk_ref, wvt_ref, wf_ref, q_ref, k_ref, vt_ref, f_ref):
    xb = x_ref[...].astype(BF16)
    q = jnp.dot(xb, wq_ref[...], preferred_element_type=F32)
    q_ref[...] = (q * ATT_SCALE).astype(BF16)
    k_ref[...] = jnp.dot(xb, wk_ref[...], preferred_element_type=F32).astype(BF16)
    vt_ref[0] = lax.dot_general(wvt_ref[...], xb, (((1,), (1,)), ((), ())),
                                preferred_element_type=F32).astype(BF16)
    f_ref[...] = jnp.dot(xb, wf_ref[...], preferred_element_type=F32)


def _fox_proj(h, B, S, wq, wk, wvt, wf, tm=512):
    T, D = h.shape
    HD = wq.shape[1]
    spt = S // tm
    return pl.pallas_call(
        _fox_proj_kernel,
        grid=(T // tm,),
        in_specs=[pl.BlockSpec((tm, D), lambda i: (i, 0)),
                  _const_spec((D, HD)), _const_spec((D, HD)), _const_spec((HD, D)),
                  _const_spec((D, LANES))],
        out_specs=[pl.BlockSpec((tm, HD), lambda i: (i, 0)),
                   pl.BlockSpec((tm, HD), lambda i: (i, 0)),
                   pl.BlockSpec((1, HD, tm), lambda i: (i // spt, 0, i % spt)),
                   pl.BlockSpec((tm, LANES), lambda i: (i, 0))],
        out_shape=[jax.ShapeDtypeStruct((T, HD), BF16), jax.ShapeDtypeStruct((T, HD), BF16),
                   jax.ShapeDtypeStruct((B, HD, S), BF16), jax.ShapeDtypeStruct((T, LANES), F32)],
        compiler_params=_params("parallel"),
        name="fox_proj",
    )(h, wq, wk, wvt, wf)


FOX_AUG_K = 0
FOX_AUG_Q = 3


def _cum_kernel(f_ref, bf_ref, qa_ref, ka_ref, carry_ref):
    tc = f_ref.shape[1]
    W = qa_ref.shape[2]

    @pl.when(pl.program_id(1) == 0)
    def _():
        carry_ref[...] = jnp.zeros_like(carry_ref)

    x = f_ref[0] + bf_ref[...]
    log_f = -_softplus(-x)
    row = lax.broadcasted_iota(jnp.int32, (tc, tc), 0)
    col = lax.broadcasted_iota(jnp.int32, (tc, tc), 1)
    cum = _dot3_left((row >= col).astype(BF16), log_f) + carry_ref[0:1, :]
    carry_ref[...] = jnp.broadcast_to(cum[tc - 1:tc, :], carry_ref.shape)

    hh = lax.broadcasted_iota(jnp.int32, (LANES, W), 0)
    cc = lax.broadcasted_iota(jnp.int32, (LANES, W), 1)
    base = (hh >> 1) * LANES + (hh & 1) * ATT_HEAD_DIM
    valid = hh < ATT_HEADS
    within = lax.broadcasted_iota(jnp.int32, (tc, W), 1) & (ATT_HEAD_DIM - 1)
    parts = _split3(cum)
    qa = jnp.where(jnp.logical_and(within >= FOX_AUG_K, within < FOX_AUG_K + 3), -1.0, 0.0)
    ka = jnp.where(jnp.logical_and(within >= FOX_AUG_Q, within < FOX_AUG_Q + 3), 1.0, 0.0)
    for c, part in enumerate(parts):
        pk = jnp.logical_and(valid, cc == base + (FOX_AUG_K + c)).astype(BF16)
        pq = jnp.logical_and(valid, cc == base + (FOX_AUG_Q + c)).astype(BF16)
        ka = ka + jnp.dot(part, pk, preferred_element_type=F32)
        qa = qa + jnp.dot(part, pq, preferred_element_type=F32)
    qa_ref[0] = qa.astype(BF16)
    ka_ref[0] = ka.astype(BF16)


def _fox_cum(f, b_f, width, tc=256):
    B, S, W = f.shape
    return pl.pallas_call(
        _cum_kernel,
        grid=(B, S // tc),
        in_specs=[pl.BlockSpec((1, tc, W), lambda b, c: (b, c, 0)), _const_spec((1, W))],
        out_specs=[pl.BlockSpec((1, tc, width), lambda b, c: (b, c, 0)),
                   pl.BlockSpec((1, tc, width), lambda b, c: (b, c, 0))],
        out_shape=[jax.ShapeDtypeStruct((B, S, width), BF16), jax.ShapeDtypeStruct((B, S, width), BF16)],
        scratch_shapes=[pltpu.VMEM((8, W), F32)],
        compiler_params=_params("arbitrary", "arbitrary"),
        name="fox_cum",
    )(f, _pad_lanes(b_f.reshape(1, -1), W))


def _fox_kernel(q_ref, qa_ref, k_ref, ka_ref, vt_ref, o_ref, m_ref, l_ref, acc_ref, *, t):
    qi = pl.program_id(2)
    q2 = jnp.concatenate([q_ref[0], qa_ref[0]], axis=1)
    masks = _head_masks(t, 2 * LANES)
    qcat = jnp.concatenate([jnp.where(masks[e], q2, jnp.zeros_like(q2)) for e in range(2)], axis=0)
    kpos = lax.broadcasted_iota(jnp.int32, (t, ATT_GROUP), 0)
    qcol = lax.broadcasted_iota(jnp.int32, (t, ATT_GROUP), 1)

    def causal_mask(c, s):
        return jnp.where(kpos <= qcol + (c * ATT_GROUP) % t, s, NEG_INF)

    def kv_step(j, diag):
        ks = pl.multiple_of(j * t, t)
        k2 = jnp.concatenate([k_ref[0, pl.ds(ks, t), :], ka_ref[0, pl.ds(ks, t), :]], axis=1)
        _attn_step(k2, qcat, vt_ref[0, :, pl.ds(ks, t)], causal_mask if diag else (lambda c, s: s),
                   m_ref, l_ref, acc_ref, t, first=diag)

    kv_step(qi, True)

    def body(j, carry):
        kv_step(j, False)
        return carry

    lax.fori_loop(0, qi, body, 0)
    _finish_pair(o_ref, l_ref, acc_ref, t)


def _fox_attention_core(q, qa, k, ka, vt, t=512):
    B, S, D = q.shape
    kern = functools.partial(_fox_kernel, t=t)
    tile = pl.BlockSpec((1, t, LANES), lambda b, hp, i: (b, i, hp))
    full = pl.BlockSpec((1, S, LANES), lambda b, hp, i: (b, 0, hp))
    return pl.pallas_call(
        kern,
        grid=(B, D // LANES, S // t),
        in_specs=[tile, tile, full, full,
                  pl.BlockSpec((1, LANES, S), lambda b, hp, i: (b, hp, 0))],
        out_specs=tile,
        out_shape=jax.ShapeDtypeStruct((B, S, D), BF16),
        scratch_shapes=[
            pltpu.VMEM((1, 2 * t), F32), pltpu.VMEM((1, 2 * t), F32),
            pltpu.VMEM((LANES, 2 * t), F32),
        ],
        compiler_params=_params("parallel", "parallel", "arbitrary"),
        name="fox_attention",
    )(q, qa, k, ka, vt)


def _fox_mixer(h, B, S, w_in, b_f, w_out, ln_g, ln_b):
    HD = ATT_HEADS * ATT_HEAD_DIM
    wq = w_in[:, :HD].astype(BF16)
    wk = w_in[:, HD:2 * HD].astype(BF16)
    wvt = w_in[:, 2 * HD:3 * HD].T.astype(BF16)
    wf = _pad_lanes(w_in[:, 3 * HD:]).astype(BF16)
    q, k, vt, f = _fox_proj(h, B, S, wq, wk, wvt, wf)
    qa, ka = _fox_cum(f.reshape(B, S, LANES), b_f, HD)
    o = _fox_attention_core(q.reshape(B, S, HD), qa, k.reshape(B, S, HD), ka, vt)
    return _outproj_ln(o.reshape(B * S, HD), w_out.astype(BF16), h, ln_g, ln_b)


def _moba_proj_kernel(x_ref, wq_ref, wk_ref, wvt_ref, q_ref, k_ref, vt_ref):
    xb = x_ref[...].astype(BF16)
    q_ref[...] = jnp.dot(xb, wq_ref[...], preferred_element_type=F32)
    k_ref[...] = jnp.dot(xb, wk_ref[...], preferred_element_type=F32).astype(BF16)
    vt_ref[0] = lax.dot_general(wvt_ref[...], xb, (((1,), (1,)), ((), ())),
                                preferred_element_type=F32).astype(BF16)


def _moba_proj(h, B, S, wq, wk, wvt, tm=512):
    T, D = h.shape
    HD = wq.shape[1]
    spt = S // tm
    return pl.pallas_call(
        _moba_proj_kernel,
        grid=(T // tm,),
        in_specs=[pl.BlockSpec((tm, D), lambda i: (i, 0)),
                  _const_spec((D, HD)), _const_spec((D, HD)), _const_spec((HD, D))],
        out_specs=[pl.BlockSpec((tm, HD), lambda i: (i, 0)),
                   pl.BlockSpec((tm, HD), lambda i: (i, 0)),
                   pl.BlockSpec((1, HD, tm), lambda i: (i // spt, 0, i % spt))],
        out_shape=[jax.ShapeDtypeStruct((T, HD), F32), jax.ShapeDtypeStruct((T, HD), BF16),
                   jax.ShapeDtypeStruct((B, HD, S), BF16)],
        compiler_params=_params("parallel"),
        name="moba_proj",
    )(h, wq, wk, wvt)


def _moba_kernel(qf_ref, k_ref, vt_ref, o_ref, m_ref, l_ref, acc_ref, km_ref, bias_ref, *, nb):
    L = MOBA_BLOCK
    qt = pl.program_id(2)

    @pl.when(qt == 0)
    def _():
        for n in range(nb):
            kb = k_ref[0, n * L:(n + 1) * L, :].astype(F32)
            km_ref[n:n + 1, :] = jnp.sum(kb, 0, keepdims=True) * (1.0 / L)

    t = 2 * L
    masks = _head_masks(t, LANES)
    qf = qf_ref[0]
    qfcat = jnp.concatenate([jnp.where(masks[e], qf, jnp.zeros_like(qf)) for e in range(2)], axis=0)
    qcat = (qfcat * ATT_SCALE).astype(BF16)
    colq = lax.broadcasted_iota(jnp.int32, (1, 2 * t), 1) & (t - 1)
    second = colq >= L
    own = (2 * qt + second.astype(jnp.int32)).astype(F32)
    blk = lax.broadcasted_iota(jnp.int32, (nb, 2 * t), 0).astype(F32)

    gate = lax.dot_general(km_ref[...], qfcat, (((1,), (1,)), ((), ())), preferred_element_type=F32,
                           precision=lax.Precision.HIGHEST)
    gate = jnp.where(blk < own, gate, NEG_INF)
    bias = jnp.full((nb, 2 * t), NEG_INF, F32)
    for _ in range(MOBA_TOPK):
        mx = jnp.max(gate, 0, keepdims=True)
        first = jnp.min(jnp.where(gate == mx, blk, float(nb)), 0, keepdims=True)
        pick = jnp.logical_and(blk == first, mx > NEG_INF)
        bias = jnp.where(pick, 0.0, bias)
        gate = jnp.where(pick, NEG_INF, gate)
    bias_ref[...] = bias

    g = ATT_GROUP
    kpos = lax.broadcasted_iota(jnp.int32, (t, g), 0)
    qcol = lax.broadcasted_iota(jnp.int32, (t, g), 1)

    def own_adjust(c, s):
        s = jnp.where(kpos <= qcol + (c * g) % t, s, NEG_INF)
        past = jnp.where(second[:, c * g:(c + 1) * g], bias_ref[pl.ds(2 * qt, 1), c * g:(c + 1) * g], 0.0)
        return jnp.concatenate([s[0:L] + past, s[L:t]], axis=0)

    ks = pl.multiple_of(qt * t, t)
    _attn_step(k_ref[0, pl.ds(ks, t), :], qcat, vt_ref[0, :, pl.ds(ks, t)], own_adjust,
               m_ref, l_ref, acc_ref, t, first=True)

    def body(j, carry):
        def past_adjust(c, s):
            b0 = jnp.broadcast_to(bias_ref[pl.ds(2 * j, 1), c * g:(c + 1) * g], (L, g))
            b1 = jnp.broadcast_to(bias_ref[pl.ds(2 * j + 1, 1), c * g:(c + 1) * g], (L, g))
            return s + jnp.concatenate([b0, b1], axis=0)

        ks = pl.multiple_of(j * t, t)
        _attn_step(k_ref[0, pl.ds(ks, t), :], qcat, vt_ref[0, :, pl.ds(ks, t)], past_adjust,
                   m_ref, l_ref, acc_ref, t)
        return carry

    lax.fori_loop(0, qt, body, 0)
    _finish_pair(o_ref, l_ref, acc_ref, t)


def _moba_attention_core(qf, k, vt):
    B, S, D = qf.shape
    t = 2 * MOBA_BLOCK
    nb = S // MOBA_BLOCK
    tile = pl.BlockSpec((1, t, LANES), lambda b, hp, i: (b, i, hp))
    return pl.pallas_call(
        functools.partial(_moba_kernel, nb=nb),
        grid=(B, D // LANES, S // t),
        in_specs=[tile,
                  pl.BlockSpec((1, S, LANES), lambda b, hp, i: (b, 0, hp)),
                  pl.BlockSpec((1, LANES, S), lambda b, hp, i: (b, hp, 0))],
        out_specs=tile,
        out_shape=jax.ShapeDtypeStruct((B, S, D), BF16),
        scratch_shapes=[
            pltpu.VMEM((1, 2 * t), F32), pltpu.VMEM((1, 2 * t), F32),
            pltpu.VMEM((LANES, 2 * t), F32),
            pltpu.VMEM((nb, LANES), F32), pltpu.VMEM((nb, 2 * t), F32),
        ],
        compiler_params=_params("parallel", "arbitrary", "arbitrary"),
        name="moba_attention",
    )(qf, k, vt)


def _moba_mixer(h, B, S, w_in, w_out, ln_g, ln_b):
    HD = ATT_HEADS * ATT_HEAD_DIM
    wq = w_in[:, :HD].astype(BF16)
    wk = w_in[:, HD:2 * HD].astype(BF16)
    wvt = w_in[:, 2 * HD:3 * HD].T.astype(BF16)
    qf, k, vt = _moba_proj(h, B, S, wq, wk, wvt)
    o = _moba_attention_core(qf.reshape(B, S, HD), k.reshape(B, S, HD), vt)
    return _outproj_ln(o.reshape(B * S, HD), w_out.astype(BF16), h, ln_g, ln_b)


def kernel(x, ffn_w_gate, ffn_w_up, ffn_w_down, ln_g, ln_b, ssm_w_in, ssm_conv_w, ssm_conv_b,
           ssm_dt_bias, ssm_a_log, ssm_d, ssm_norm_w, ssm_w_out, fox_w_in, fox_b_f, fox_w_out,
           moba_w_in, moba_w_out):
    B, S, D = x.shape
    h = x.reshape(B * S, D)
    for layer in range(DEPTH):
        kind, j = layer % 3, layer // 3
        h = _ffn_ln(h, ffn_w_gate[layer, 0].astype(BF16), ffn_w_up[layer, 0].astype(BF16),
                    ffn_w_down[layer, 0].astype(BF16), ln_g[layer, 0], ln_b[layer, 0])
        if kind == 0:
            h = _mamba2_mixer(h, B, S, ssm_w_in[j], ssm_conv_w[j], ssm_conv_b[j], ssm_dt_bias[j],
                              ssm_a_log[j], ssm_d[j], ssm_norm_w[j], ssm_w_out[j],
                              ln_g[layer, 1], ln_b[layer, 1])
        elif kind == 1:
            h = _fox_mixer(h, B, S, fox_w_in[j], fox_b_f[j], fox_w_out[j],
                           ln_g[layer, 1], ln_b[layer, 1])
        else:
            h = _moba_mixer(h, B, S, moba_w_in[j], moba_w_out[j], ln_g[layer, 1], ln_b[layer, 1])
        h = _ffn_ln(h, ffn_w_gate[layer, 1].astype(BF16), ffn_w_up[layer, 1].astype(BF16),
                    ffn_w_down[layer, 1].astype(BF16), ln_g[layer, 2], ln_b[layer, 2])
    return h.reshape(B, S, D)
```

```python
import functools

import jax
import jax.numpy as jnp
from jax import lax
from jax.experimental import pallas as pl
from jax.experimental.pallas import tpu as pltpu

F32 = jnp.float32
BF16 = jnp.bfloat16

DEPTH = 4
D_MODEL = 1024
D_FF = 2816
LN_EPS = 1e-5
DEEPNORM_ALPHA = (2 * DEPTH) ** 0.25
MACARON_WEIGHT = 0.5

SSM_D_INNER = 2048
SSM_HEAD_DIM = 64
SSM_HEADS = 32
SSM_GROUPS = 4
SSM_HEADS_PER_GROUP = 8
SSM_D_STATE = 128
SSM_CONV = 4
SSM_CHUNK = 256
SSM_BC_DIM = SSM_GROUPS * SSM_D_STATE
SSM_CONV_DIM = SSM_D_INNER + 2 * SSM_BC_DIM
SSM_GROUP_WIDTH = SSM_D_INNER // SSM_GROUPS

ATT_HEAD_DIM = 64
ATT_HEADS = 16
ATT_SCALE = ATT_HEAD_DIM ** -0.5
MOBA_BLOCK = 256
MOBA_TOPK = 3

LANES = 128
NEG_INF = float("-inf")
VMEM_LIMIT = 56 * 1024 * 1024


def _params(*sem):
    return pltpu.CompilerParams(dimension_semantics=sem, vmem_limit_bytes=VMEM_LIMIT)


def _const_spec(shape):
    nd = len(shape)
    return pl.BlockSpec(shape, lambda *_: (0,) * nd, pipeline_mode=pl.Buffered(1))


def _layer_norm(v, g, b):
    mu = jnp.mean(v, -1, keepdims=True)
    d = v - mu
    var = jnp.mean(d * d, -1, keepdims=True)
    return d * lax.rsqrt(var + LN_EPS) * g + b


def _split3(v):
    hi = v.astype(BF16)
    r1 = v - hi.astype(F32)
    mid = r1.astype(BF16)
    lo = (r1 - mid.astype(F32)).astype(BF16)
    return hi, mid, lo


def _dot3(v, m01):
    out = None
    for part in _split3(v):
        t = jnp.dot(part, m01, preferred_element_type=F32)
        out = t if out is None else out + t
    return out


def _dot3_left(m01, v):
    out = None
    for part in _split3(v):
        t = jnp.dot(m01, part, preferred_element_type=F32)
        out = t if out is None else out + t
    return out


def _softplus(x):
    return jnp.maximum(x, 0.0) + jnp.log1p(jnp.exp(-jnp.abs(x)))


def _silu(x):
    return x * jax.nn.sigmoid(x)


def _pad_lanes(v, n=LANES):
    return jnp.pad(v, [(0, 0)] * (v.ndim - 1) + [(0, n - v.shape[-1])])


def _ffn_kernel(h_ref, wg_ref, wu_ref, wd_ref, g_ref, b_ref, o_ref):
    h = h_ref[...]
    hb = h.astype(BF16)
    gate = jnp.dot(hb, wg_ref[...], preferred_element_type=F32)
    up = jnp.dot(hb, wu_ref[...], preferred_element_type=F32)
    act = (_silu(gate) * up).astype(BF16)
    ff = jnp.dot(act, wd_ref[...], preferred_element_type=F32)
    o_ref[...] = _layer_norm(DEEPNORM_ALPHA * h + MACARON_WEIGHT * ff, g_ref[...], b_ref[...])


def _ffn_ln(h, wg, wu, wd, g, b, tm=512):
    T, D = h.shape
    FF = wg.shape[1]
    return pl.pallas_call(
        _ffn_kernel,
        grid=(T // tm,),
        in_specs=[
            pl.BlockSpec((tm, D), lambda i: (i, 0)),
            _const_spec((D, FF)), _const_spec((D, FF)), _const_spec((FF, D)),
            _const_spec((1, D)), _const_spec((1, D)),
        ],
        out_specs=pl.BlockSpec((tm, D), lambda i: (i, 0)),
        out_shape=jax.ShapeDtypeStruct((T, D), F32),
        compiler_params=_params("parallel"),
        name="ffn_ln",
    )(h, wg, wu, wd, g.reshape(1, D), b.reshape(1, D))


def _ssd_proj_kernel(x_ref, wz_ref, wx_ref, wdt_ref, z_ref, xbc_ref, dt_ref):
    xb = x_ref[...].astype(BF16)
    z_ref[...] = jnp.dot(xb, wz_ref[...], preferred_element_type=F32)
    xbc_ref[...] = jnp.dot(xb, wx_ref[...], preferred_element_type=F32)
    dt_ref[...] = jnp.dot(xb, wdt_ref[...], preferred_element_type=F32)


def _ssd_proj(x, wz, wx, wdt, tm=512):
    T, D = x.shape
    widths = (wz.shape[1], wx.shape[1], wdt.shape[1])
    return pl.pallas_call(
        _ssd_proj_kernel,
        grid=(T // tm,),
        in_specs=[pl.BlockSpec((tm, D), lambda i: (i, 0))] + [_const_spec((D, n)) for n in widths],
        out_specs=[pl.BlockSpec((tm, n), lambda i: (i, 0)) for n in widths],
        out_shape=[jax.ShapeDtypeStruct((T, n), F32) for n in widths],
        compiler_params=_params("parallel"),
        name="ssd_proj",
    )(x, wz, wx, wdt)


def _outproj_kernel(y_ref, w_ref, h_ref, g_ref, b_ref, o_ref):
    mix = jnp.dot(y_ref[...].astype(BF16), w_ref[...], preferred_element_type=F32)
    o_ref[...] = _layer_norm(DEEPNORM_ALPHA * h_ref[...] + mix, g_ref[...], b_ref[...])


def _outproj_ln(y, w, h, g, b, tm=512):
    T, K = y.shape
    D = w.shape[1]
    return pl.pallas_call(
        _outproj_kernel,
        grid=(T // tm,),
        in_specs=[
            pl.BlockSpec((tm, K), lambda i: (i, 0)), _const_spec((K, D)),
            pl.BlockSpec((tm, D), lambda i: (i, 0)),
            _const_spec((1, D)), _const_spec((1, D)),
        ],
        out_specs=pl.BlockSpec((tm, D), lambda i: (i, 0)),
        out_shape=jax.ShapeDtypeStruct((T, D), F32),
        compiler_params=_params("parallel"),
        name="outproj_ln",
    )(y, w, h, g.reshape(1, D), b.reshape(1, D))


def _ssd_kernel(z_ref, xbc_ref, dt_ref, cw_ref, cb_ref, dtb_ref, alog_ref, dskip_ref, nw_ref,
                y_ref, pad_ref, state_ref, ydiag_ref):
    Q = SSM_CHUNK
    c = pl.program_id(1)

    @pl.when(c == 0)
    def _():
        pad_ref[0:8, :] = jnp.zeros((8, SSM_CONV_DIM), F32)
        state_ref[...] = jnp.zeros_like(state_ref)

    xraw = xbc_ref[0]
    pad_ref[8:8 + Q, :] = xraw
    conv = cb_ref[...] + cw_ref[SSM_CONV - 1:SSM_CONV, :] * xraw
    for k in range(1, SSM_CONV):
        conv = conv + cw_ref[SSM_CONV - 1 - k:SSM_CONV - k, :] * pad_ref[pl.ds(8 - k, Q), :]
    pad_ref[0:8, :] = xraw[Q - 8:Q, :]
    xbc = _silu(conv)
    xs = xbc[:, :SSM_D_INNER]

    dt = _softplus(dt_ref[0] + dtb_ref[...])
    adt = dt * (-jnp.exp(alog_ref[...]))
    row = lax.broadcasted_iota(jnp.int32, (Q, Q), 0)
    col = lax.broadcasted_iota(jnp.int32, (Q, Q), 1)
    tril = row >= col
    cs = _dot3_left(tril.astype(BF16), adt)
    cs_t = cs.T
    cs_last = cs[Q - 1:Q, :]

    eh = lax.broadcasted_iota(jnp.int32, (LANES, SSM_D_INNER), 0)
    ec = lax.broadcasted_iota(jnp.int32, (LANES, SSM_D_INNER), 1)
    expand = (ec // SSM_HEAD_DIM == eh).astype(BF16)
    xdt = xs * _dot3(dt, expand)
    dec_out = _dot3(jnp.exp(cs), expand)
    dec_end = _dot3(jnp.exp(cs_last - cs), expand)
    dec_chunk = _dot3(jnp.broadcast_to(jnp.exp(cs_last), (8, LANES)), expand)[0:1, :]

    xdt_b = xdt.astype(BF16)
    xend_b = (xdt * dec_end).astype(BF16)
    lane = lax.broadcasted_iota(jnp.int32, (Q, LANES), 1)
    low_half = lane < SSM_HEAD_DIM

    for g in range(SSM_GROUPS):
        bg = xbc[:, SSM_D_INNER + g * SSM_D_STATE:SSM_D_INNER + (g + 1) * SSM_D_STATE]
        cg = xbc[:, SSM_D_INNER + SSM_BC_DIM + g * SSM_D_STATE:
                 SSM_D_INNER + SSM_BC_DIM + (g + 1) * SSM_D_STATE]
        bg_b = bg.astype(BF16)
        cg_b = cg.astype(BF16)
        cb = lax.dot_general(cg_b, bg_b, (((1,), (1,)), ((), ())), preferred_element_type=F32)
        for pr in range(SSM_HEADS_PER_GROUP // 2):
            pair = g * (SSM_HEADS_PER_GROUP // 2) + pr
            xp = xdt_b[:, pair * LANES:(pair + 1) * LANES]
            acc = None
            for e in range(2):
                h = 2 * pair + e
                seg = cs[:, h:h + 1] - cs_t[h:h + 1, :]
                m = (cb * jnp.exp(jnp.where(tril, seg, NEG_INF))).astype(BF16)
                xm = jnp.where(low_half if e == 0 else jnp.logical_not(low_half), xp, jnp.zeros_like(xp))
                t = jnp.dot(m, xm, preferred_element_type=F32)
                acc = t if acc is None else acc + t
            ydiag_ref[:, pair * LANES:(pair + 1) * LANES] = acc
        gs = slice(g * SSM_GROUP_WIDTH, (g + 1) * SSM_GROUP_WIDTH)
        st = state_ref[g]
        y_off = jnp.dot(cg_b, st.astype(BF16), preferred_element_type=F32) * dec_out[:, gs]
        ydiag_ref[:, gs] = ydiag_ref[:, gs] + y_off
        upd = jnp.dot(bg.T.astype(BF16), xend_b[:, gs], preferred_element_type=F32)
        state_ref[g] = st * dec_chunk[:, gs] + upd

    y = ydiag_ref[...] + xs * dskip_ref[...]
    y = y * _silu(z_ref[0])
    for g in range(SSM_GROUPS):
        gs = slice(g * SSM_GROUP_WIDTH, (g + 1) * SSM_GROUP_WIDTH)
        yg = y[:, gs]
        yg = yg * lax.rsqrt(jnp.mean(yg * yg, -1, keepdims=True) + LN_EPS)
        y_ref[0, :, gs] = (yg * nw_ref[:, gs]).astype(y_ref.dtype)


def _ssd_core(z, xbc, dt, conv_w, conv_b, dt_bias, a_log, d_skip, norm_w):
    B, S, _ = z.shape
    Q = SSM_CHUNK
    nc = S // Q
    blk = lambda w: pl.BlockSpec((1, Q, w), lambda b, c: (b, c, 0))
    return pl.pallas_call(
        _ssd_kernel,
        grid=(B, nc),
        in_specs=[
            blk(SSM_D_INNER), blk(SSM_CONV_DIM), blk(LANES),
            _const_spec((SSM_CONV, SSM_CONV_DIM)), _const_spec((1, SSM_CONV_DIM)),
            _const_spec((1, LANES)), _const_spec((1, LANES)),
            _const_spec((1, SSM_D_INNER)), _const_spec((1, SSM_D_INNER)),
        ],
        out_specs=blk(SSM_D_INNER),
        out_shape=jax.ShapeDtypeStruct((B, S, SSM_D_INNER), BF16),
        scratch_shapes=[
            pltpu.VMEM((8 + Q, SSM_CONV_DIM), F32),
            pltpu.VMEM((SSM_GROUPS, SSM_D_STATE, SSM_GROUP_WIDTH), F32),
            pltpu.VMEM((Q, SSM_D_INNER), F32),
        ],
        compiler_params=_params("arbitrary", "arbitrary"),
        name="ssd_core",
    )(z, xbc, dt, conv_w, conv_b.reshape(1, -1), _pad_lanes(dt_bias.reshape(1, -1)),
      _pad_lanes(a_log.reshape(1, -1)),
      jnp.repeat(d_skip, SSM_HEAD_DIM).reshape(1, -1), norm_w.reshape(1, -1))


def _mamba2_mixer(h, B, S, w_in, conv_w, conv_b, dt_bias, a_log, d_skip, norm_w, w_out, ln_g, ln_b):
    wz = w_in[:, :SSM_D_INNER].astype(BF16)
    wx = w_in[:, SSM_D_INNER:SSM_D_INNER + SSM_CONV_DIM].astype(BF16)
    wdt = _pad_lanes(w_in[:, SSM_D_INNER + SSM_CONV_DIM:]).astype(BF16)
    z, xbc, dt = _ssd_proj(h, wz, wx, wdt)
    y = _ssd_core(z.reshape(B, S, -1), xbc.reshape(B, S, -1), dt.reshape(B, S, -1),
                  conv_w, conv_b, dt_bias, a_log, d_skip, norm_w)
    return _outproj_ln(y.reshape(B * S, -1), w_out.astype(BF16), h, ln_g, ln_b)


def _head_masks(n_rows, width):
    lane = lax.broadcasted_iota(jnp.int32, (n_rows, width), 1)
    low = (lane & ATT_HEAD_DIM) == 0
    return [low, jnp.logical_not(low)]


LOG2E = 1.4426950408889634


def _pv_update(p_ref, a_ref, vt, acc_ref, tq):
    d = ATT_HEAD_DIM
    for e in range(2):
        rows = slice(e * d, (e + 1) * d)
        cols = slice(e * tq, (e + 1) * tq)
        acc_ref[rows, cols] = a_ref[:, cols] * acc_ref[rows, cols] + jnp.dot(
            vt[rows], p_ref[:, cols], preferred_element_type=F32)


def _qk_softmax(k2, qcat, adjust, m_ref, l_ref, p_ref, a_ref, tq, g, first=False, pending=None):
    n_groups = 2 * tq // g

    def qk(c):
        return lax.dot_general(k2, qcat[c * g:(c + 1) * g], (((1,), (1,)), ((), ())),
                               preferred_element_type=F32)

    s_next = qk(0)
    if pending is not None:
        pending()
    for c in range(n_groups):
        s = s_next
        if c + 1 < n_groups:
            s_next = qk(c + 1)
        s = adjust(c, s)
        cols = slice(c * g, (c + 1) * g)
        s_max = jnp.max(s, 0, keepdims=True)
        if first:
            m_new = s_max
            p = jnp.exp2(s - m_new)
            l_ref[:, cols] = jnp.sum(p, 0, keepdims=True)
            a_ref[:, cols] = jnp.ones_like(m_new)
        else:
            m_old = m_ref[:, cols]
            m_new = jnp.maximum(m_old, s_max)
            a = jnp.exp2(m_old - m_new)
            p = jnp.exp2(s - m_new)
            l_ref[:, cols] = a * l_ref[:, cols] + jnp.sum(p, 0, keepdims=True)
            a_ref[:, cols] = a
        p_ref[:, cols] = p.astype(BF16)
        m_ref[:, cols] = m_new


def _finish_pair(o_ref, l_ref, acc_ref, tq):
    d = ATT_HEAD_DIM
    ot = jnp.concatenate([acc_ref[0:d, 0:tq] / l_ref[:, 0:tq],
                          acc_ref[d:2 * d, tq:2 * tq] / l_ref[:, tq:2 * tq]], axis=0)
    o_ref[0] = ot.T.astype(o_ref.dtype)


def _fox_proj_kernel(x_ref, wq_ref, wk_ref, wvt_ref, wf_ref, q_ref, k_ref, vt_ref, f_ref):
    xb = x_ref[...].astype(BF16)
    q = jnp.dot(xb, wq_ref[...], preferred_element_type=F32)
    q_ref[...] = (q * (ATT_SCALE * LOG2E)).astype(BF16)
    k_ref[...] = jnp.dot(xb, wk_ref[...], preferred_element_type=F32).astype(BF16)
    vt_ref[0] = lax.dot_general(wvt_ref[...], xb, (((1,), (1,)), ((), ())),
                                preferred_element_type=F32).astype(BF16)
    f_ref[...] = jnp.dot(xb, wf_ref[...], preferred_element_type=F32)


def _fox_proj(h, B, S, wq, wk, wvt, wf, tm=512):
    T, D = h.shape
    HD = wq.shape[1]
    spt = S // tm
    return pl.pallas_call(
        _fox_proj_kernel,
        grid=(T // tm,),
        in_specs=[pl.BlockSpec((tm, D), lambda i: (i, 0)),
                  _const_spec((D, HD)), _const_spec((D, HD)), _const_spec((HD, D)),
                  _const_spec((D, LANES))],
        out_specs=[pl.BlockSpec((tm, HD), lambda i: (i, 0)),
                   pl.BlockSpec((tm, HD), lambda i: (i, 0)),
                   pl.BlockSpec((1, HD, tm), lambda i: (i // spt, 0, i % spt)),
                   pl.BlockSpec((tm, LANES), lambda i: (i, 0))],
        out_shape=[jax.ShapeDtypeStruct((T, HD), BF16), jax.ShapeDtypeStruct((T, HD), BF16),
                   jax.ShapeDtypeStruct((B, HD, S), BF16), jax.ShapeDtypeStruct((T, LANES), F32)],
        compiler_params=_params("parallel"),
        name="fox_proj",
    )(h, wq, wk, wvt, wf)


FOX_AUG_K = 0
FOX_AUG_Q = 3


def _cum_kernel(f_ref, bf_ref, qa_ref, ka_ref, carry_ref):
    tc = f_ref.shape[1]
    W = qa_ref.shape[2]

    @pl.when(pl.program_id(1) == 0)
    def _():
        carry_ref[...] = jnp.zeros_like(carry_ref)

    x = f_ref[0] + bf_ref[...]
    log_f = -_softplus(-x)
    row = lax.broadcasted_iota(jnp.int32, (tc, tc), 0)
    col = lax.broadcasted_iota(jnp.int32, (tc, tc), 1)
    cum = _dot3_left((row >= col).astype(BF16), log_f) + carry_ref[0:1, :]
    carry_ref[...] = jnp.broadcast_to(cum[tc - 1:tc, :], carry_ref.shape)

    hh = lax.broadcasted_iota(jnp.int32, (LANES, W), 0)
    cc = lax.broadcasted_iota(jnp.int32, (LANES, W), 1)
    base = (hh >> 1) * LANES + (hh & 1) * ATT_HEAD_DIM
    valid = hh < ATT_HEADS
    within = lax.broadcasted_iota(jnp.int32, (tc, W), 1) & (ATT_HEAD_DIM - 1)
    parts = _split3(cum * LOG2E)
    qa = jnp.where(jnp.logical_and(within >= FOX_AUG_K, within < FOX_AUG_K + 3), -1.0, 0.0)
    ka = jnp.where(jnp.logical_and(within >= FOX_AUG_Q, within < FOX_AUG_Q + 3), 1.0, 0.0)
    for c, part in enumerate(parts):
        pk = jnp.logical_and(valid, cc == base + (FOX_AUG_K + c)).astype(BF16)
        pq = jnp.logical_and(valid, cc == base + (FOX_AUG_Q + c)).astype(BF16)
        ka = ka + jnp.dot(part, pk, preferred_element_type=F32)
        qa = qa + jnp.dot(part, pq, preferred_element_type=F32)
    qa_ref[0] = qa.astype(BF16)
    ka_ref[0] = ka.astype(BF16)


def _fox_cum(f, b_f, width, tc=256):
    B, S, W = f.shape
    return pl.pallas_call(
        _cum_kernel,
        grid=(B, S // tc),
        in_specs=[pl.BlockSpec((1, tc, W), lambda b, c: (b, c, 0)), _const_spec((1, W))],
        out_specs=[pl.BlockSpec((1, tc, width), lambda b, c: (b, c, 0)),
                   pl.BlockSpec((1, tc, width), lambda b, c: (b, c, 0))],
        out_shape=[jax.ShapeDtypeStruct((B, S, width), BF16), jax.ShapeDtypeStruct((B, S, width), BF16)],
        scratch_shapes=[pltpu.VMEM((8, W), F32)],
        compiler_params=_params("arbitrary", "arbitrary"),
        name="fox_cum",
    )(f, _pad_lanes(b_f.reshape(1, -1), W))


def _fox_kernel(q_ref, qa_ref, k_ref, ka_ref, vt_ref, o_ref, m_ref, l_ref, acc_ref, p_ref, a_ref, *, t, g):
    qi = pl.program_id(2)
    q2 = jnp.concatenate([q_ref[0], qa_ref[0]], axis=1)
    masks = _head_masks(t, 2 * LANES)
    qcat = jnp.concatenate([jnp.where(masks[e], q2, jnp.zeros_like(q2)) for e in range(2)], axis=0)
    kpos = lax.broadcasted_iota(jnp.int32, (t, g), 0)
    qcol = lax.broadcasted_iota(jnp.int32, (t, g), 1)

    def causal_mask(c, s):
        return jnp.where(kpos <= ((qcol + c * g) & (t - 1)), s, NEG_INF)

    def keys(j):
        ks = pl.multiple_of(j * t, t)
        return jnp.concatenate([k_ref[0, pl.ds(ks, t), :], ka_ref[0, pl.ds(ks, t), :]], axis=1)

    def pv(j):
        _pv_update(p_ref, a_ref, vt_ref[0, :, pl.ds(pl.multiple_of(j * t, t), t)], acc_ref, t)

    acc_ref[...] = jnp.zeros_like(acc_ref)
    _qk_softmax(keys(qi), qcat, causal_mask, m_ref, l_ref, p_ref, a_ref, t, g, first=True)

    def body(j, prev):
        _qk_softmax(keys(j), qcat, lambda c, s: s, m_ref, l_ref, p_ref, a_ref, t, g,
                    pending=lambda: pv(prev))
        return j

    last = lax.fori_loop(0, qi, body, qi)
    pv(last)
    _finish_pair(o_ref, l_ref, acc_ref, t)


def _fox_attention_core(q, qa, k, ka, vt, t=512, g=512):
    B, S, D = q.shape
    kern = functools.partial(_fox_kernel, t=t, g=g)
    tile = pl.BlockSpec((1, t, LANES), lambda b, hp, i: (b, i, hp))
    full = pl.BlockSpec((1, S, LANES), lambda b, hp, i: (b, 0, hp))
    return pl.pallas_call(
        kern,
        grid=(B, D // LANES, S // t),
        in_specs=[tile, tile, full, full,
                  pl.BlockSpec((1, LANES, S), lambda b, hp, i: (b, hp, 0))],
        out_specs=tile,
        out_shape=jax.ShapeDtypeStruct((B, S, D), BF16),
        scratch_shapes=[
            pltpu.VMEM((1, 2 * t), F32), pltpu.VMEM((1, 2 * t), F32),
            pltpu.VMEM((LANES, 2 * t), F32),
            pltpu.VMEM((t, 2 * t), BF16), pltpu.VMEM((1, 2 * t), F32),
        ],
        compiler_params=_params("parallel", "parallel", "arbitrary"),
        name="fox_attention",
    )(q, qa, k, ka, vt)


def _fox_mixer(h, B, S, w_in, b_f, w_out, ln_g, ln_b):
    HD = ATT_HEADS * ATT_HEAD_DIM
    wq = w_in[:, :HD].astype(BF16)
    wk = w_in[:, HD:2 * HD].astype(BF16)
    wvt = w_in[:, 2 * HD:3 * HD].T.astype(BF16)
    wf = _pad_lanes(w_in[:, 3 * HD:]).astype(BF16)
    q, k, vt, f = _fox_proj(h, B, S, wq, wk, wvt, wf)
    qa, ka = _fox_cum(f.reshape(B, S, LANES), b_f, HD)
    o = _fox_attention_core(q.reshape(B, S, HD), qa, k.reshape(B, S, HD), ka, vt)
    return _outproj_ln(o.reshape(B * S, HD), w_out.astype(BF16), h, ln_g, ln_b)


def _moba_proj_kernel(x_ref, wq_ref, wk_ref, wvt_ref, q_ref, k_ref, vt_ref):
    xb = x_ref[...].astype(BF16)
    q_ref[...] = jnp.dot(xb, wq_ref[...], preferred_element_type=F32)
    k_ref[...] = jnp.dot(xb, wk_ref[...], preferred_element_type=F32).astype(BF16)
    vt_ref[0] = lax.dot_general(wvt_ref[...], xb, (((1,), (1,)), ((), ())),
                                preferred_element_type=F32).astype(BF16)


def _moba_proj(h, B, S, wq, wk, wvt, tm=512):
    T, D = h.shape
    HD = wq.shape[1]
    spt = S // tm
    return pl.pallas_call(
        _moba_proj_kernel,
        grid=(T // tm,),
        in_specs=[pl.BlockSpec((tm, D), lambda i: (i, 0)),
                  _const_spec((D, HD)), _const_spec((D, HD)), _const_spec((HD, D))],
        out_specs=[pl.BlockSpec((tm, HD), lambda i: (i, 0)),
                   pl.BlockSpec((tm, HD), lambda i: (i, 0)),
                   pl.BlockSpec((1, HD, tm), lambda i: (i // spt, 0, i % spt))],
        out_shape=[jax.ShapeDtypeStruct((T, HD), F32), jax.ShapeDtypeStruct((T, HD), BF16),
                   jax.ShapeDtypeStruct((B, HD, S), BF16)],
        compiler_params=_params("parallel"),
        name="moba_proj",
    )(h, wq, wk, wvt)


def _moba_kernel(qf_ref, k_ref, vt_ref, o_ref, m_ref, l_ref, acc_ref, p_ref, a_ref, km_ref, bias_ref,
                 *, nb, g):
    L = MOBA_BLOCK
    qt = pl.program_id(2)

    @pl.when(qt == 0)
    def _():
        for n in range(nb):
            kb = k_ref[0, n * L:(n + 1) * L, :].astype(F32)
            km_ref[n:n + 1, :] = jnp.sum(kb, 0, keepdims=True) * (1.0 / L)

    t = 2 * L
    masks = _head_masks(t, LANES)
    qf = qf_ref[0]
    qfcat = jnp.concatenate([jnp.where(masks[e], qf, jnp.zeros_like(qf)) for e in range(2)], axis=0)
    qcat = (qfcat * (ATT_SCALE * LOG2E)).astype(BF16)
    colq = lax.broadcasted_iota(jnp.int32, (1, 2 * t), 1) & (t - 1)
    second = colq >= L
    own = (2 * qt + second.astype(jnp.int32)).astype(F32)
    blk = lax.broadcasted_iota(jnp.int32, (nb, 2 * t), 0).astype(F32)

    gate = lax.dot_general(km_ref[...], qfcat, (((1,), (1,)), ((), ())), preferred_element_type=F32,
                           precision=lax.Precision.HIGHEST)
    gate = jnp.where(blk < own, gate, NEG_INF)
    bias = jnp.full((nb, 2 * t), NEG_INF, F32)
    for _ in range(MOBA_TOPK):
        mx = jnp.max(gate, 0, keepdims=True)
        first = jnp.min(jnp.where(gate == mx, blk, float(nb)), 0, keepdims=True)
        pick = jnp.logical_and(blk == first, mx > NEG_INF)
        bias = jnp.where(pick, 0.0, bias)
        gate = jnp.where(pick, NEG_INF, gate)
    bias_ref[...] = bias

    kpos = lax.broadcasted_iota(jnp.int32, (t, g), 0)
    qcol = lax.broadcasted_iota(jnp.int32, (t, g), 1)

    def own_adjust(c, s):
        s = jnp.where(kpos <= ((qcol + c * g) & (t - 1)), s, NEG_INF)
        past = jnp.where(second[:, c * g:(c + 1) * g], bias_ref[pl.ds(2 * qt, 1), c * g:(c + 1) * g], 0.0)
        return jnp.concatenate([s[0:L] + past, s[L:t]], axis=0)

    def keys(j):
        return k_ref[0, pl.ds(pl.multiple_of(j * t, t), t), :]

    def pv(j):
        _pv_update(p_ref, a_ref, vt_ref[0, :, pl.ds(pl.multiple_of(j * t, t), t)], acc_ref, t)

    acc_ref[...] = jnp.zeros_like(acc_ref)
    _qk_softmax(keys(qt), qcat, own_adjust, m_ref, l_ref, p_ref, a_ref, t, g, first=True)

    def body(j, prev):
        def past_adjust(c, s):
            b0 = jnp.broadcast_to(bias_ref[pl.ds(2 * j, 1), c * g:(c + 1) * g], (L, g))
            b1 = jnp.broadcast_to(bias_ref[pl.ds(2 * j + 1, 1), c * g:(c + 1) * g], (L, g))
            return s + jnp.concatenate([b0, b1], axis=0)

        _qk_softmax(keys(j), qcat, past_adjust, m_ref, l_ref, p_ref, a_ref, t, g,
                    pending=lambda: pv(prev))
        return j

    last = lax.fori_loop(0, qt, body, qt)
    pv(last)
    _finish_pair(o_ref, l_ref, acc_ref, t)


def _moba_attention_core(qf, k, vt, g=1024):
    B, S, D = qf.shape
    t = 2 * MOBA_BLOCK
    nb = S // MOBA_BLOCK
    tile = pl.BlockSpec((1, t, LANES), lambda b, hp, i: (b, i, hp))
    return pl.pallas_call(
        functools.partial(_moba_kernel, nb=nb, g=g),
        grid=(B, D // LANES, S // t),
        in_specs=[tile,
                  pl.BlockSpec((1, S, LANES), lambda b, hp, i: (b, 0, hp)),
                  pl.BlockSpec((1, LANES, S), lambda b, hp, i: (b, hp, 0))],
        out_specs=tile,
        out_shape=jax.ShapeDtypeStruct((B, S, D), BF16),
        scratch_shapes=[
            pltpu.VMEM((1, 2 * t), F32), pltpu.VMEM((1, 2 * t), F32),
            pltpu.VMEM((LANES, 2 * t), F32),
            pltpu.VMEM((t, 2 * t), BF16), pltpu.VMEM((1, 2 * t), F32),
            pltpu.VMEM((nb, LANES), F32), pltpu.VMEM((nb, 2 * t), F32),
        ],
        compiler_params=_params("parallel", "arbitrary", "arbitrary"),
        name="moba_attention",
    )(qf, k, vt)


def _moba_mixer(h, B, S, w_in, w_out, ln_g, ln_b):
    HD = ATT_HEADS * ATT_HEAD_DIM
    wq = w_in[:, :HD].astype(BF16)
    wk = w_in[:, HD:2 * HD].astype(BF16)
    wvt = w_in[:, 2 * HD:3 * HD].T.astype(BF16)
    qf, k, vt = _moba_proj(h, B, S, wq, wk, wvt)
    o = _moba_attention_core(qf.reshape(B, S, HD), k.reshape(B, S, HD), vt)
    return _outproj_ln(o.reshape(B * S, HD), w_out.astype(BF16), h, ln_g, ln_b)


def kernel(x, ffn_w_gate, ffn_w_up, ffn_w_down, ln_g, ln_b, ssm_w_in, ssm_conv_w, ssm_conv_b,
           ssm_dt_bias, ssm_a_log, ssm_d, ssm_norm_w, ssm_w_out, fox_w_in, fox_b_f, fox_w_out,
           moba_w_in, moba_w_out):
    B, S, D = x.shape
    h = x.reshape(B * S, D)
    for layer in range(DEPTH):
        kind, j = layer % 3, layer // 3
        h = _ffn_ln(h, ffn_w_gate[layer, 0].astype(BF16), ffn_w_up[layer, 0].astype(BF16),
                    ffn_w_down[layer, 0].astype(BF16), ln_g[layer, 0], ln_b[layer, 0])
        if kind == 0:
            h = _mamba2_mixer(h, B, S, ssm_w_in[j], ssm_conv_w[j], ssm_conv_b[j], ssm_dt_bias[j],
                              ssm_a_log[j], ssm_d[j], ssm_norm_w[j], ssm_w_out[j],
                              ln_g[layer, 1], ln_b[layer, 1])
        elif kind == 1:
            h = _fox_mixer(h, B, S, fox_w_in[j], fox_b_f[j], fox_w_out[j],
                           ln_g[layer, 1], ln_b[layer, 1])
        else:
            h = _moba_mixer(h, B, S, moba_w_in[j], moba_w_out[j], ln_g[layer, 1], ln_b[layer, 1])
        h = _ffn_ln(h, ffn_w_gate[layer, 1].astype(BF16), ffn_w_up[layer, 1].astype(BF16),
                    ffn_w_down[layer, 1].astype(BF16), ln_g[layer, 2], ln_b[layer, 2])
    return h.reshape(B, S, D)
```

```python
import functools

import jax
import jax.numpy as jnp
from jax import lax
from jax.experimental import pallas as pl
from jax.experimental.pallas import tpu as pltpu

F32 = jnp.float32
BF16 = jnp.bfloat16

DEPTH = 4
D_MODEL = 1024
D_FF = 2816
LN_EPS = 1e-5
DEEPNORM_ALPHA = (2 * DEPTH) ** 0.25
MACARON_WEIGHT = 0.5

SSM_D_INNER = 2048
SSM_HEAD_DIM = 64
SSM_HEADS = 32
SSM_GROUPS = 4
SSM_HEADS_PER_GROUP = 8
SSM_D_STATE = 128
SSM_CONV = 4
SSM_CHUNK = 256
SSM_BC_DIM = SSM_GROUPS * SSM_D_STATE
SSM_CONV_DIM = SSM_D_INNER + 2 * SSM_BC_DIM
SSM_GROUP_WIDTH = SSM_D_INNER // SSM_GROUPS

ATT_HEAD_DIM = 64
ATT_HEADS = 16
ATT_SCALE = ATT_HEAD_DIM ** -0.5
MOBA_BLOCK = 256
MOBA_TOPK = 3

LANES = 128
NEG_INF = float("-inf")
VMEM_LIMIT = 56 * 1024 * 1024


def _params(*sem):
    return pltpu.CompilerParams(dimension_semantics=sem, vmem_limit_bytes=VMEM_LIMIT)


def _const_spec(shape):
    nd = len(shape)
    return pl.BlockSpec(shape, lambda *_: (0,) * nd, pipeline_mode=pl.Buffered(1))


def _layer_norm(v, g, b):
    mu = jnp.mean(v, -1, keepdims=True)
    d = v - mu
    var = jnp.mean(d * d, -1, keepdims=True)
    return d * lax.rsqrt(var + LN_EPS) * g + b


def _split3(v):
    hi = v.astype(BF16)
    r1 = v - hi.astype(F32)
    mid = r1.astype(BF16)
    lo = (r1 - mid.astype(F32)).astype(BF16)
    return hi, mid, lo


def _dot3(v, m01):
    out = None
    for part in _split3(v):
        t = jnp.dot(part, m01, preferred_element_type=F32)
        out = t if out is None else out + t
    return out


def _dot3_left(m01, v):
    out = None
    for part in _split3(v):
        t = jnp.dot(m01, part, preferred_element_type=F32)
        out = t if out is None else out + t
    return out


def _softplus(x):
    return jnp.maximum(x, 0.0) + jnp.log1p(jnp.exp(-jnp.abs(x)))


def _silu(x):
    return x * jax.nn.sigmoid(x)


def _pad_lanes(v, n=LANES):
    return jnp.pad(v, [(0, 0)] * (v.ndim - 1) + [(0, n - v.shape[-1])])


def _ffn_kernel(h_ref, wg_ref, wu_ref, wd_ref, g_ref, b_ref, o_ref):
    h = h_ref[...]
    hb = h.astype(BF16)
    gate = jnp.dot(hb, wg_ref[...], preferred_element_type=F32)
    up = jnp.dot(hb, wu_ref[...], preferred_element_type=F32)
    act = (_silu(gate) * up).astype(BF16)
    ff = jnp.dot(act, wd_ref[...], preferred_element_type=F32)
    o_ref[...] = _layer_norm(DEEPNORM_ALPHA * h + MACARON_WEIGHT * ff, g_ref[...], b_ref[...])


def _ffn_ln(h, wg, wu, wd, g, b, tm=512):
    T, D = h.shape
    FF = wg.shape[1]
    return pl.pallas_call(
        _ffn_kernel,
        grid=(T // tm,),
        in_specs=[
            pl.BlockSpec((tm, D), lambda i: (i, 0)),
            _const_spec((D, FF)), _const_spec((D, FF)), _const_spec((FF, D)),
            _const_spec((1, D)), _const_spec((1, D)),
        ],
        out_specs=pl.BlockSpec((tm, D), lambda i: (i, 0)),
        out_shape=jax.ShapeDtypeStruct((T, D), F32),
        compiler_params=_params("parallel"),
        name="ffn_ln",
    )(h, wg, wu, wd, g.reshape(1, D), b.reshape(1, D))


def _ssd_proj_kernel(x_ref, wz_ref, wx_ref, wdt_ref, z_ref, xbc_ref, dt_ref):
    xb = x_ref[...].astype(BF16)
    z_ref[...] = jnp.dot(xb, wz_ref[...], preferred_element_type=F32)
    xbc_ref[...] = jnp.dot(xb, wx_ref[...], preferred_element_type=F32)
    dt_ref[...] = jnp.dot(xb, wdt_ref[...], preferred_element_type=F32)


def _ssd_proj(x, wz, wx, wdt, tm=512):
    T, D = x.shape
    widths = (wz.shape[1], wx.shape[1], wdt.shape[1])
    return pl.pallas_call(
        _ssd_proj_kernel,
        grid=(T // tm,),
        in_specs=[pl.BlockSpec((tm, D), lambda i: (i, 0))] + [_const_spec((D, n)) for n in widths],
        out_specs=[pl.BlockSpec((tm, n), lambda i: (i, 0)) for n in widths],
        out_shape=[jax.ShapeDtypeStruct((T, n), F32) for n in widths],
        compiler_params=_params("parallel"),
        name="ssd_proj",
    )(x, wz, wx, wdt)


def _outproj_kernel(y_ref, w_ref, h_ref, g_ref, b_ref, o_ref):
    mix = jnp.dot(y_ref[...].astype(BF16), w_ref[...], preferred_element_type=F32)
    o_ref[...] = _layer_norm(DEEPNORM_ALPHA * h_ref[...] + mix, g_ref[...], b_ref[...])


def _outproj_ln(y, w, h, g, b, tm=512):
    T, K = y.shape
    D = w.shape[1]
    return pl.pallas_call(
        _outproj_kernel,
        grid=(T // tm,),
        in_specs=[
            pl.BlockSpec((tm, K), lambda i: (i, 0)), _const_spec((K, D)),
            pl.BlockSpec((tm, D), lambda i: (i, 0)),
            _const_spec((1, D)), _const_spec((1, D)),
        ],
        out_specs=pl.BlockSpec((tm, D), lambda i: (i, 0)),
        out_shape=jax.ShapeDtypeStruct((T, D), F32),
        compiler_params=_params("parallel"),
        name="outproj_ln",
    )(y, w, h, g.reshape(1, D), b.reshape(1, D))


def _ssd_kernel(z_ref, xbc_ref, dt_ref, cw_ref, cb_ref, dtb_ref, alog_ref, dskip_ref, nw_ref,
                y_ref, pad_ref, state_ref, ydiag_ref):
    Q = SSM_CHUNK
    c = pl.program_id(1)

    @pl.when(c == 0)
    def _():
        pad_ref[0:8, :] = jnp.zeros((8, SSM_CONV_DIM), F32)
        state_ref[...] = jnp.zeros_like(state_ref)

    xraw = xbc_ref[0]
    pad_ref[8:8 + Q, :] = xraw
    conv = cb_ref[...] + cw_ref[SSM_CONV - 1:SSM_CONV, :] * xraw
    for k in range(1, SSM_CONV):
        conv = conv + cw_ref[SSM_CONV - 1 - k:SSM_CONV - k, :] * pad_ref[pl.ds(8 - k, Q), :]
    pad_ref[0:8, :] = xraw[Q - 8:Q, :]
    xbc = _silu(conv)
    xs = xbc[:, :SSM_D_INNER]

    dt = _softplus(dt_ref[0] + dtb_ref[...])
    adt = dt * (-jnp.exp(alog_ref[...]))
    row = lax.broadcasted_iota(jnp.int32, (Q, Q), 0)
    col = lax.broadcasted_iota(jnp.int32, (Q, Q), 1)
    tril = row >= col
    cs = _dot3_left(tril.astype(BF16), adt)
    cs_t = cs.T
    cs_last = cs[Q - 1:Q, :]

    eh = lax.broadcasted_iota(jnp.int32, (LANES, SSM_D_INNER), 0)
    ec = lax.broadcasted_iota(jnp.int32, (LANES, SSM_D_INNER), 1)
    expand = (ec // SSM_HEAD_DIM == eh).astype(BF16)
    xdt = xs * _dot3(dt, expand)
    dec_out = _dot3(jnp.exp(cs), expand)
    dec_end = _dot3(jnp.exp(cs_last - cs), expand)
    dec_chunk = _dot3(jnp.broadcast_to(jnp.exp(cs_last), (8, LANES)), expand)[0:1, :]

    xdt_b = xdt.astype(BF16)
    xend_b = (xdt * dec_end).astype(BF16)
    lane = lax.broadcasted_iota(jnp.int32, (Q, LANES), 1)
    low_half = lane < SSM_HEAD_DIM

    for g in range(SSM_GROUPS):
        bg = xbc[:, SSM_D_INNER + g * SSM_D_STATE:SSM_D_INNER + (g + 1) * SSM_D_STATE]
        cg = xbc[:, SSM_D_INNER + SSM_BC_DIM + g * SSM_D_STATE:
                 SSM_D_INNER + SSM_BC_DIM + (g + 1) * SSM_D_STATE]
        bg_b = bg.astype(BF16)
        cg_b = cg.astype(BF16)
        cb = lax.dot_general(cg_b, bg_b, (((1,), (1,)), ((), ())), preferred_element_type=F32)
        for pr in range(SSM_HEADS_PER_GROUP // 2):
            pair = g * (SSM_HEADS_PER_GROUP // 2) + pr
            xp = xdt_b[:, pair * LANES:(pair + 1) * LANES]
            acc = None
            for e in range(2):
                h = 2 * pair + e
                seg = cs[:, h:h + 1] - cs_t[h:h + 1, :]
                m = (cb * jnp.exp(jnp.where(tril, seg, NEG_INF))).astype(BF16)
                xm = jnp.where(low_half if e == 0 else jnp.logical_not(low_half), xp, jnp.zeros_like(xp))
                t = jnp.dot(m, xm, preferred_element_type=F32)
                acc = t if acc is None else acc + t
            ydiag_ref[:, pair * LANES:(pair + 1) * LANES] = acc
        gs = slice(g * SSM_GROUP_WIDTH, (g + 1) * SSM_GROUP_WIDTH)
        st = state_ref[g]
        y_off = jnp.dot(cg_b, st.astype(BF16), preferred_element_type=F32) * dec_out[:, gs]
        ydiag_ref[:, gs] = ydiag_ref[:, gs] + y_off
        upd = jnp.dot(bg.T.astype(BF16), xend_b[:, gs], preferred_element_type=F32)
        state_ref[g] = st * dec_chunk[:, gs] + upd

    y = ydiag_ref[...] + xs * dskip_ref[...]
    y = y * _silu(z_ref[0])
    for g in range(SSM_GROUPS):
        gs = slice(g * SSM_GROUP_WIDTH, (g + 1) * SSM_GROUP_WIDTH)
        yg = y[:, gs]
        yg = yg * lax.rsqrt(jnp.mean(yg * yg, -1, keepdims=True) + LN_EPS)
        y_ref[0, :, gs] = (yg * nw_ref[:, gs]).astype(y_ref.dtype)


def _ssd_core(z, xbc, dt, conv_w, conv_b, dt_bias, a_log, d_skip, norm_w):
    B, S, _ = z.shape
    Q = SSM_CHUNK
    nc = S // Q
    blk = lambda w: pl.BlockSpec((1, Q, w), lambda b, c: (b, c, 0))
    return pl.pallas_call(
        _ssd_kernel,
        grid=(B, nc),
        in_specs=[
            blk(SSM_D_INNER), blk(SSM_CONV_DIM), blk(LANES),
            _const_spec((SSM_CONV, SSM_CONV_DIM)), _const_spec((1, SSM_CONV_DIM)),
            _const_spec((1, LANES)), _const_spec((1, LANES)),
            _const_spec((1, SSM_D_INNER)), _const_spec((1, SSM_D_INNER)),
        ],
        out_specs=blk(SSM_D_INNER),
        out_shape=jax.ShapeDtypeStruct((B, S, SSM_D_INNER), BF16),
        scratch_shapes=[
            pltpu.VMEM((8 + Q, SSM_CONV_DIM), F32),
            pltpu.VMEM((SSM_GROUPS, SSM_D_STATE, SSM_GROUP_WIDTH), F32),
            pltpu.VMEM((Q, SSM_D_INNER), F32),
        ],
        compiler_params=_params("arbitrary", "arbitrary"),
        name="ssd_core",
    )(z, xbc, dt, conv_w, conv_b.reshape(1, -1), _pad_lanes(dt_bias.reshape(1, -1)),
      _pad_lanes(a_log.reshape(1, -1)),
      jnp.repeat(d_skip, SSM_HEAD_DIM).reshape(1, -1), norm_w.reshape(1, -1))


def _mamba2_mixer(h, B, S, w_in, conv_w, conv_b, dt_bias, a_log, d_skip, norm_w, w_out, ln_g, ln_b):
    wz = w_in[:, :SSM_D_INNER].astype(BF16)
    wx = w_in[:, SSM_D_INNER:SSM_D_INNER + SSM_CONV_DIM].astype(BF16)
    wdt = _pad_lanes(w_in[:, SSM_D_INNER + SSM_CONV_DIM:]).astype(BF16)
    z, xbc, dt = _ssd_proj(h, wz, wx, wdt)
    y = _ssd_core(z.reshape(B, S, -1), xbc.reshape(B, S, -1), dt.reshape(B, S, -1),
                  conv_w, conv_b, dt_bias, a_log, d_skip, norm_w)
    return _outproj_ln(y.reshape(B * S, -1), w_out.astype(BF16), h, ln_g, ln_b)


def _head_masks(n_rows, width):
    lane = lax.broadcasted_iota(jnp.int32, (n_rows, width), 1)
    low = (lane & ATT_HEAD_DIM) == 0
    return [low, jnp.logical_not(low)]


LOG2E = 1.4426950408889634


def _pv_update(p_ref, a_ref, vt, acc_ref, tq):
    d = ATT_HEAD_DIM
    for e in range(2):
        rows = slice(e * d, (e + 1) * d)
        cols = slice(e * tq, (e + 1) * tq)
        acc_ref[rows, cols] = a_ref[:, cols] * acc_ref[rows, cols] + jnp.dot(
            vt[rows], p_ref[:, cols], preferred_element_type=F32)


def _qk_softmax(k2, qcat, adjust, m_ref, l_ref, p_ref, a_ref, tq, g, first=False, pending=None):
    n_groups = 2 * tq // g

    def qk(c):
        return lax.dot_general(k2, qcat[c * g:(c + 1) * g], (((1,), (1,)), ((), ())),
                               preferred_element_type=F32)

    s_next = qk(0)
    if pending is not None:
        pending()
    for c in range(n_groups):
        s = s_next
        if c + 1 < n_groups:
            s_next = qk(c + 1)
        s = adjust(c, s)
        cols = slice(c * g, (c + 1) * g)
        s_max = jnp.max(s, 0, keepdims=True)
        if first:
            m_new = s_max
            p = jnp.exp2(s - m_new)
            l_ref[:, cols] = jnp.sum(p, 0, keepdims=True)
            a_ref[:, cols] = jnp.ones_like(m_new)
        else:
            m_old = m_ref[:, cols]
            m_new = jnp.maximum(m_old, s_max)
            a = jnp.exp2(m_old - m_new)
            p = jnp.exp2(s - m_new)
            l_ref[:, cols] = a * l_ref[:, cols] + jnp.sum(p, 0, keepdims=True)
            a_ref[:, cols] = a
        p_ref[:, cols] = p.astype(BF16)
        m_ref[:, cols] = m_new


def _finish_pair(o_ref, l_ref, acc_ref, tq):
    d = ATT_HEAD_DIM
    ot = jnp.concatenate([acc_ref[0:d, 0:tq] / l_ref[:, 0:tq],
                          acc_ref[d:2 * d, tq:2 * tq] / l_ref[:, tq:2 * tq]], axis=0)
    o_ref[0] = ot.T.astype(o_ref.dtype)


def _fox_proj_kernel(x_ref, wq_ref, wk_ref, wvt_ref, wf_ref, q_ref, k_ref, vt_ref, f_ref):
    xb = x_ref[...].astype(BF16)
    q = jnp.dot(xb, wq_ref[...], preferred_element_type=F32)
    q_ref[...] = (q * (ATT_SCALE * LOG2E)).astype(BF16)
    k_ref[...] = jnp.dot(xb, wk_ref[...], preferred_element_type=F32).astype(BF16)
    vt_ref[0] = lax.dot_general(wvt_ref[...], xb, (((1,), (1,)), ((), ())),
                                preferred_element_type=F32).astype(BF16)
    f_ref[...] = jnp.dot(xb, wf_ref[...], preferred_element_type=F32)


def _fox_proj(h, B, S, wq, wk, wvt, wf, tm=512):
    T, D = h.shape
    HD = wq.shape[1]
    spt = S // tm
    return pl.pallas_call(
        _fox_proj_kernel,
        grid=(T // tm,),
        in_specs=[pl.BlockSpec((tm, D), lambda i: (i, 0)),
                  _const_spec((D, HD)), _const_spec((D, HD)), _const_spec((HD, D)),
                  _const_spec((D, LANES))],
        out_specs=[pl.BlockSpec((tm, HD), lambda i: (i, 0)),
                   pl.BlockSpec((tm, HD), lambda i: (i, 0)),
                   pl.BlockSpec((1, HD, tm), lambda i: (i // spt, 0, i % spt)),
                   pl.BlockSpec((tm, LANES), lambda i: (i, 0))],
        out_shape=[jax.ShapeDtypeStruct((T, HD), BF16), jax.ShapeDtypeStruct((T, HD), BF16),
                   jax.ShapeDtypeStruct((B, HD, S), BF16), jax.ShapeDtypeStruct((T, LANES), F32)],
        compiler_params=_params("parallel"),
        name="fox_proj",
    )(h, wq, wk, wvt, wf)


FOX_AUG_K = 0
FOX_AUG_Q = 3


def _cum_kernel(f_ref, bf_ref, qa_ref, ka_ref, carry_ref):
    tc = f_ref.shape[1]
    W = qa_ref.shape[2]

    @pl.when(pl.program_id(1) == 0)
    def _():
        carry_ref[...] = jnp.zeros_like(carry_ref)

    x = f_ref[0] + bf_ref[...]
    log_f = -_softplus(-x)
    row = lax.broadcasted_iota(jnp.int32, (tc, tc), 0)
    col = lax.broadcasted_iota(jnp.int32, (tc, tc), 1)
    cum = _dot3_left((row >= col).astype(BF16), log_f) + carry_ref[0:1, :]
    carry_ref[...] = jnp.broadcast_to(cum[tc - 1:tc, :], carry_ref.shape)

    hh = lax.broadcasted_iota(jnp.int32, (LANES, W), 0)
    cc = lax.broadcasted_iota(jnp.int32, (LANES, W), 1)
    base = (hh >> 1) * LANES + (hh & 1) * ATT_HEAD_DIM
    valid = hh < ATT_HEADS
    within = lax.broadcasted_iota(jnp.int32, (tc, W), 1) & (ATT_HEAD_DIM - 1)
    parts = _split3(cum * LOG2E)
    qa = jnp.where(jnp.logical_and(within >= FOX_AUG_K, within < FOX_AUG_K + 3), -1.0, 0.0)
    ka = jnp.where(jnp.logical_and(within >= FOX_AUG_Q, within < FOX_AUG_Q + 3), 1.0, 0.0)
    for c, part in enumerate(parts):
        pk = jnp.logical_and(valid, cc == base + (FOX_AUG_K + c)).astype(BF16)
        pq = jnp.logical_and(valid, cc == base + (FOX_AUG_Q + c)).astype(BF16)
        ka = ka + jnp.dot(part, pk, preferred_element_type=F32)
        qa = qa + jnp.dot(part, pq, preferred_element_type=F32)
    qa_ref[0] = qa.astype(BF16)
    ka_ref[0] = ka.astype(BF16)


def _fox_cum(f, b_f, width, tc=256):
    B, S, W = f.shape
    return pl.pallas_call(
        _cum_kernel,
        grid=(B, S // tc),
        in_specs=[pl.BlockSpec((1, tc, W), lambda b, c: (b, c, 0)), _const_spec((1, W))],
        out_specs=[pl.BlockSpec((1, tc, width), lambda b, c: (b, c, 0)),
                   pl.BlockSpec((1, tc, width), lambda b, c: (b, c, 0))],
        out_shape=[jax.ShapeDtypeStruct((B, S, width), BF16), jax.ShapeDtypeStruct((B, S, width), BF16)],
        scratch_shapes=[pltpu.VMEM((8, W), F32)],
        compiler_params=_params("arbitrary", "arbitrary"),
        name="fox_cum",
    )(f, _pad_lanes(b_f.reshape(1, -1), W))


def _fox_kernel(q_ref, qa_ref, k_ref, ka_ref, vt_ref, o_ref, m_ref, l_ref, acc_ref, p_ref, a_ref, *, t, g):
    qi = pl.program_id(2)
    q2 = jnp.concatenate([q_ref[0], qa_ref[0]], axis=1)
    masks = _head_masks(t, 2 * LANES)
    qcat = jnp.concatenate([jnp.where(masks[e], q2, jnp.zeros_like(q2)) for e in range(2)], axis=0)
    kpos = lax.broadcasted_iota(jnp.int32, (t, g), 0)
    qcol = lax.broadcasted_iota(jnp.int32, (t, g), 1)

    def causal_mask(c, s):
        return jnp.where(kpos <= ((qcol + c * g) & (t - 1)), s, NEG_INF)

    def keys(j):
        ks = pl.multiple_of(j * t, t)
        return jnp.concatenate([k_ref[0, pl.ds(ks, t), :], ka_ref[0, pl.ds(ks, t), :]], axis=1)

    def pv(j):
        _pv_update(p_ref, a_ref, vt_ref[0, :, pl.ds(pl.multiple_of(j * t, t), t)], acc_ref, t)

    acc_ref[...] = jnp.zeros_like(acc_ref)
    _qk_softmax(keys(qi), qcat, causal_mask, m_ref, l_ref, p_ref, a_ref, t, g, first=True)

    def body(j, prev):
        _qk_softmax(keys(j), qcat, lambda c, s: s, m_ref, l_ref, p_ref, a_ref, t, g,
                    pending=lambda: pv(prev))
        return j

    last = lax.fori_loop(0, qi, body, qi)
    pv(last)
    _finish_pair(o_ref, l_ref, acc_ref, t)


def _fox_attention_core(q, qa, k, ka, vt, t=512, g=512):
    B, S, D = q.shape
    kern = functools.partial(_fox_kernel, t=t, g=g)
    tile = pl.BlockSpec((1, t, LANES), lambda b, hp, i: (b, i, hp))
    full = pl.BlockSpec((1, S, LANES), lambda b, hp, i: (b, 0, hp))
    return pl.pallas_call(
        kern,
        grid=(B, D // LANES, S // t),
        in_specs=[tile, tile, full, full,
                  pl.BlockSpec((1, LANES, S), lambda b, hp, i: (b, hp, 0))],
        out_specs=tile,
        out_shape=jax.ShapeDtypeStruct((B, S, D), BF16),
        scratch_shapes=[
            pltpu.VMEM((1, 2 * t), F32), pltpu.VMEM((1, 2 * t), F32),
            pltpu.VMEM((LANES, 2 * t), F32),
            pltpu.VMEM((t, 2 * t), BF16), pltpu.VMEM((1, 2 * t), F32),
        ],
        compiler_params=_params("parallel", "parallel", "arbitrary"),
        name="fox_attention",
    )(q, qa, k, ka, vt)


def _fox_mixer(h, B, S, w_in, b_f, w_out, ln_g, ln_b):
    HD = ATT_HEADS * ATT_HEAD_DIM
    wq = w_in[:, :HD].astype(BF16)
    wk = w_in[:, HD:2 * HD].astype(BF16)
    wvt = w_in[:, 2 * HD:3 * HD].T.astype(BF16)
    wf = _pad_lanes(w_in[:, 3 * HD:]).astype(BF16)
    q, k, vt, f = _fox_proj(h, B, S, wq, wk, wvt, wf)
    qa, ka = _fox_cum(f.reshape(B, S, LANES), b_f, HD)
    o = _fox_attention_core(q.reshape(B, S, HD), qa, k.reshape(B, S, HD), ka, vt)
    return _outproj_ln(o.reshape(B * S, HD), w_out.astype(BF16), h, ln_g, ln_b)


def _moba_proj_kernel(x_ref, wq_ref, wk_ref, wvt_ref, q_ref, k_ref, vt_ref):
    xb = x_ref[...].astype(BF16)
    q_ref[...] = jnp.dot(xb, wq_ref[...], preferred_element_type=F32)
    k_ref[...] = jnp.dot(xb, wk_ref[...], preferred_element_type=F32).astype(BF16)
    vt_ref[0] = lax.dot_general(wvt_ref[...], xb, (((1,), (1,)), ((), ())),
                                preferred_element_type=F32).astype(BF16)


def _moba_proj(h, B, S, wq, wk, wvt, tm=512):
    T, D = h.shape
    HD = wq.shape[1]
    spt = S // tm
    return pl.pallas_call(
        _moba_proj_kernel,
        grid=(T // tm,),
        in_specs=[pl.BlockSpec((tm, D), lambda i: (i, 0)),
                  _const_spec((D, HD)), _const_spec((D, HD)), _const_spec((HD, D))],
        out_specs=[pl.BlockSpec((tm, HD), lambda i: (i, 0)),
                   pl.BlockSpec((tm, HD), lambda i: (i, 0)),
                   pl.BlockSpec((1, HD, tm), lambda i: (i // spt, 0, i % spt))],
        out_shape=[jax.ShapeDtypeStruct((T, HD), F32), jax.ShapeDtypeStruct((T, HD), BF16),
                   jax.ShapeDtypeStruct((B, HD, S), BF16)],
        compiler_params=_params("parallel"),
        name="moba_proj",
    )(h, wq, wk, wvt)


def _moba_kernel(qf_ref, k_ref, vt_ref, o_ref, m_ref, l_ref, acc_ref, p_ref, a_ref, km_ref, bias_ref,
                 *, nb, g):
    L = MOBA_BLOCK
    qt = pl.program_id(2)

    @pl.when(qt == 0)
    def _():
        for n in range(nb):
            kb = k_ref[0, n * L:(n + 1) * L, :].astype(F32)
            km_ref[n:n + 1, :] = jnp.sum(kb, 0, keepdims=True) * (1.0 / L)

    t = 2 * L
    masks = _head_masks(t, LANES)
    qf = qf_ref[0]
    qfcat = jnp.concatenate([jnp.where(masks[e], qf, jnp.zeros_like(qf)) for e in range(2)], axis=0)
    qcat = (qfcat * (ATT_SCALE * LOG2E)).astype(BF16)
    colq = lax.broadcasted_iota(jnp.int32, (1, 2 * t), 1) & (t - 1)
    second = colq >= L
    own = (2 * qt + second.astype(jnp.int32)).astype(F32)
    blk = lax.broadcasted_iota(jnp.int32, (nb, 2 * t), 0).astype(F32)

    gate = lax.dot_general(km_ref[...], qfcat, (((1,), (1,)), ((), ())), preferred_element_type=F32,
                           precision=lax.Precision.HIGHEST)
    gate = jnp.where(blk < own, gate, NEG_INF)
    bias = jnp.full((nb, 2 * t), NEG_INF, F32)
    for _ in range(MOBA_TOPK):
        mx = jnp.max(gate, 0, keepdims=True)
        first = jnp.min(jnp.where(gate == mx, blk, float(nb)), 0, keepdims=True)
        pick = jnp.logical_and(blk == first, mx > NEG_INF)
        bias = jnp.where(pick, 0.0, bias)
        gate = jnp.where(pick, NEG_INF, gate)
    bias_ref[...] = bias

    kpos = lax.broadcasted_iota(jnp.int32, (t, g), 0)
    qcol = lax.broadcasted_iota(jnp.int32, (t, g), 1)

    def own_adjust(c, s):
        s = jnp.where(kpos <= ((qcol + c * g) & (t - 1)), s, NEG_INF)
        past = jnp.where(second[:, c * g:(c + 1) * g], bias_ref[pl.ds(2 * qt, 1), c * g:(c + 1) * g], 0.0)
        return jnp.concatenate([s[0:L] + past, s[L:t]], axis=0)

    def keys(j):
        return k_ref[0, pl.ds(pl.multiple_of(j * t, t), t), :]

    def pv(j):
        _pv_update(p_ref, a_ref, vt_ref[0, :, pl.ds(pl.multiple_of(j * t, t), t)], acc_ref, t)

    acc_ref[...] = jnp.zeros_like(acc_ref)
    _qk_softmax(keys(qt), qcat, own_adjust, m_ref, l_ref, p_ref, a_ref, t, g, first=True)

    def body(j, prev):
        def past_adjust(c, s):
            b0 = bias_ref[pl.ds(2 * j, 1), c * g:(c + 1) * g]
            b1 = bias_ref[pl.ds(2 * j + 1, 1), c * g:(c + 1) * g]
            return jnp.concatenate([s[0:L] + b0, s[L:t] + b1], axis=0)

        _qk_softmax(keys(j), qcat, past_adjust, m_ref, l_ref, p_ref, a_ref, t, g,
                    pending=lambda: pv(prev))
        return j

    last = lax.fori_loop(0, qt, body, qt)
    pv(last)
    _finish_pair(o_ref, l_ref, acc_ref, t)


def _moba_attention_core(qf, k, vt, g=512):
    B, S, D = qf.shape
    t = 2 * MOBA_BLOCK
    nb = S // MOBA_BLOCK
    tile = pl.BlockSpec((1, t, LANES), lambda b, hp, i: (b, i, hp))
    return pl.pallas_call(
        functools.partial(_moba_kernel, nb=nb, g=g),
        grid=(B, D // LANES, S // t),
        in_specs=[tile,
                  pl.BlockSpec((1, S, LANES), lambda b, hp, i: (b, 0, hp)),
                  pl.BlockSpec((1, LANES, S), lambda b, hp, i: (b, hp, 0))],
        out_specs=tile,
        out_shape=jax.ShapeDtypeStruct((B, S, D), BF16),
        scratch_shapes=[
            pltpu.VMEM((1, 2 * t), F32), pltpu.VMEM((1, 2 * t), F32),
            pltpu.VMEM((LANES, 2 * t), F32),
            pltpu.VMEM((t, 2 * t), BF16), pltpu.VMEM((1, 2 * t), F32),
            pltpu.VMEM((nb, LANES), F32), pltpu.VMEM((nb, 2 * t), F32),
        ],
        compiler_params=_params("parallel", "arbitrary", "arbitrary"),
        name="moba_attention",
    )(qf, k, vt)


def _moba_mixer(h, B, S, w_in, w_out, ln_g, ln_b):
    HD = ATT_HEADS * ATT_HEAD_DIM
    wq = w_in[:, :HD].astype(BF16)
    wk = w_in[:, HD:2 * HD].astype(BF16)
    wvt = w_in[:, 2 * HD:3 * HD].T.astype(BF16)
    qf, k, vt = _moba_proj(h, B, S, wq, wk, wvt)
    o = _moba_attention_core(qf.reshape(B, S, HD), k.reshape(B, S, HD), vt)
    return _outproj_ln(o.reshape(B * S, HD), w_out.astype(BF16), h, ln_g, ln_b)


def kernel(x, ffn_w_gate, ffn_w_up, ffn_w_down, ln_g, ln_b, ssm_w_in, ssm_conv_w, ssm_conv_b,
           ssm_dt_bias, ssm_a_log, ssm_d, ssm_norm_w, ssm_w_out, fox_w_in, fox_b_f, fox_w_out,
           moba_w_in, moba_w_out):
    B, S, D = x.shape
    h = x.reshape(B * S, D)
    for layer in range(DEPTH):
        kind, j = layer % 3, layer // 3
        h = _ffn_ln(h, ffn_w_gate[layer, 0].astype(BF16), ffn_w_up[layer, 0].astype(BF16),
                    ffn_w_down[layer, 0].astype(BF16), ln_g[layer, 0], ln_b[layer, 0])
        if kind == 0:
            h = _mamba2_mixer(h, B, S, ssm_w_in[j], ssm_conv_w[j], ssm_conv_b[j], ssm_dt_bias[j],
                              ssm_a_log[j], ssm_d[j], ssm_norm_w[j], ssm_w_out[j],
                              ln_g[layer, 1], ln_b[layer, 1])
        elif kind == 1:
            h = _fox_mixer(h, B, S, fox_w_in[j], fox_b_f[j], fox_w_out[j],
                           ln_g[layer, 1], ln_b[layer, 1])
        else:
            h = _moba_mixer(h, B, S, moba_w_in[j], moba_w_out[j], ln_g[layer, 1], ln_b[layer, 1])
        h = _ffn_ln(h, ffn_w_gate[layer, 1].astype(BF16), ffn_w_up[layer, 1].astype(BF16),
                    ffn_w_down[layer, 1].astype(BF16), ln_g[layer, 2], ln_b[layer, 2])
    return h.reshape(B, S, D)
```

```python
import functools

import jax
import jax.numpy as jnp
from jax import lax
from jax.experimental import pallas as pl
from jax.experimental.pallas import tpu as pltpu

F32 = jnp.float32
BF16 = jnp.bfloat16

DEPTH = 4
D_MODEL = 1024
D_FF = 2816
LN_EPS = 1e-5
DEEPNORM_ALPHA = (2 * DEPTH) ** 0.25
MACARON_WEIGHT = 0.5

SSM_D_INNER = 2048
SSM_HEAD_DIM = 64
SSM_HEADS = 32
SSM_GROUPS = 4
SSM_HEADS_PER_GROUP = 8
SSM_D_STATE = 128
SSM_CONV = 4
SSM_CHUNK = 256
SSM_BC_DIM = SSM_GROUPS * SSM_D_STATE
SSM_CONV_DIM = SSM_D_INNER + 2 * SSM_BC_DIM
SSM_GROUP_WIDTH = SSM_D_INNER // SSM_GROUPS

ATT_HEAD_DIM = 64
ATT_HEADS = 16
ATT_SCALE = ATT_HEAD_DIM ** -0.5
MOBA_BLOCK = 256
MOBA_TOPK = 3

LANES = 128
NEG_INF = float("-inf")
VMEM_LIMIT = 56 * 1024 * 1024


def _params(*sem):
    return pltpu.CompilerParams(dimension_semantics=sem, vmem_limit_bytes=VMEM_LIMIT)


def _const_spec(shape):
    nd = len(shape)
    return pl.BlockSpec(shape, lambda *_: (0,) * nd, pipeline_mode=pl.Buffered(1))


def _layer_norm(v, g, b):
    mu = jnp.mean(v, -1, keepdims=True)
    d = v - mu
    var = jnp.mean(d * d, -1, keepdims=True)
    return d * lax.rsqrt(var + LN_EPS) * g + b


def _split3(v):
    hi = v.astype(BF16)
    r1 = v - hi.astype(F32)
    mid = r1.astype(BF16)
    lo = (r1 - mid.astype(F32)).astype(BF16)
    return hi, mid, lo


def _dot3(v, m01):
    out = None
    for part in _split3(v):
        t = jnp.dot(part, m01, preferred_element_type=F32)
        out = t if out is None else out + t
    return out


def _dot3_left(m01, v):
    out = None
    for part in _split3(v):
        t = jnp.dot(m01, part, preferred_element_type=F32)
        out = t if out is None else out + t
    return out


def _softplus(x):
    return jnp.maximum(x, 0.0) + jnp.log1p(jnp.exp(-jnp.abs(x)))


def _silu(x):
    return x * jax.nn.sigmoid(x)


def _pad_lanes(v, n=LANES):
    return jnp.pad(v, [(0, 0)] * (v.ndim - 1) + [(0, n - v.shape[-1])])


def _ffn_kernel(h_ref, wg_ref, wu_ref, wd_ref, g_ref, b_ref, o_ref):
    h = h_ref[...]
    hb = h.astype(BF16)
    gate = jnp.dot(hb, wg_ref[...], preferred_element_type=F32)
    up = jnp.dot(hb, wu_ref[...], preferred_element_type=F32)
    act = (_silu(gate) * up).astype(BF16)
    ff = jnp.dot(act, wd_ref[...], preferred_element_type=F32)
    o_ref[...] = _layer_norm(DEEPNORM_ALPHA * h + MACARON_WEIGHT * ff, g_ref[...], b_ref[...])


def _ffn_ln(h, wg, wu, wd, g, b, tm=512):
    T, D = h.shape
    FF = wg.shape[1]
    return pl.pallas_call(
        _ffn_kernel,
        grid=(T // tm,),
        in_specs=[
            pl.BlockSpec((tm, D), lambda i: (i, 0)),
            _const_spec((D, FF)), _const_spec((D, FF)), _const_spec((FF, D)),
            _const_spec((1, D)), _const_spec((1, D)),
        ],
        out_specs=pl.BlockSpec((tm, D), lambda i: (i, 0)),
        out_shape=jax.ShapeDtypeStruct((T, D), F32),
        compiler_params=_params("parallel"),
        name="ffn_ln",
    )(h, wg, wu, wd, g.reshape(1, D), b.reshape(1, D))


def _ssd_proj_kernel(x_ref, wz_ref, wx_ref, wdt_ref, z_ref, xbc_ref, dt_ref):
    xb = x_ref[...].astype(BF16)
    z_ref[...] = jnp.dot(xb, wz_ref[...], preferred_element_type=F32)
    xbc_ref[...] = jnp.dot(xb, wx_ref[...], preferred_element_type=F32)
    dt_ref[...] = jnp.dot(xb, wdt_ref[...], preferred_element_type=F32)


def _ssd_proj(x, wz, wx, wdt, tm=512):
    T, D = x.shape
    widths = (wz.shape[1], wx.shape[1], wdt.shape[1])
    return pl.pallas_call(
        _ssd_proj_kernel,
        grid=(T // tm,),
        in_specs=[pl.BlockSpec((tm, D), lambda i: (i, 0))] + [_const_spec((D, n)) for n in widths],
        out_specs=[pl.BlockSpec((tm, n), lambda i: (i, 0)) for n in widths],
        out_shape=[jax.ShapeDtypeStruct((T, n), F32) for n in widths],
        compiler_params=_params("parallel"),
        name="ssd_proj",
    )(x, wz, wx, wdt)


def _outproj_kernel(y_ref, w_ref, h_ref, g_ref, b_ref, o_ref):
    mix = jnp.dot(y_ref[...].astype(BF16), w_ref[...], preferred_element_type=F32)
    o_ref[...] = _layer_norm(DEEPNORM_ALPHA * h_ref[...] + mix, g_ref[...], b_ref[...])


def _outproj_ln(y, w, h, g, b, tm=512):
    T, K = y.shape
    D = w.shape[1]
    return pl.pallas_call(
        _outproj_kernel,
        grid=(T // tm,),
        in_specs=[
            pl.BlockSpec((tm, K), lambda i: (i, 0)), _const_spec((K, D)),
            pl.BlockSpec((tm, D), lambda i: (i, 0)),
            _const_spec((1, D)), _const_spec((1, D)),
        ],
        out_specs=pl.BlockSpec((tm, D), lambda i: (i, 0)),
        out_shape=jax.ShapeDtypeStruct((T, D), F32),
        compiler_params=_params("parallel"),
        name="outproj_ln",
    )(y, w, h, g.reshape(1, D), b.reshape(1, D))


def _ssd_kernel(z_ref, xbc_ref, dt_ref, cw_ref, cb_ref, dtb_ref, alog_ref, dskip_ref, nw_ref,
                y_ref, pad_ref, state_ref, ydiag_ref):
    Q = SSM_CHUNK
    c = pl.program_id(1)

    @pl.when(c == 0)
    def _():
        pad_ref[0:8, :] = jnp.zeros((8, SSM_CONV_DIM), F32)
        state_ref[...] = jnp.zeros_like(state_ref)

    xraw = xbc_ref[0]
    pad_ref[8:8 + Q, :] = xraw
    conv = cb_ref[...] + cw_ref[SSM_CONV - 1:SSM_CONV, :] * xraw
    for k in range(1, SSM_CONV):
        conv = conv + cw_ref[SSM_CONV - 1 - k:SSM_CONV - k, :] * pad_ref[pl.ds(8 - k, Q), :]
    pad_ref[0:8, :] = xraw[Q - 8:Q, :]
    xbc = _silu(conv)
    xs = xbc[:, :SSM_D_INNER]

    dt = _softplus(dt_ref[0] + dtb_ref[...])
    adt = dt * (-jnp.exp(alog_ref[...]))
    row = lax.broadcasted_iota(jnp.int32, (Q, Q), 0)
    col = lax.broadcasted_iota(jnp.int32, (Q, Q), 1)
    tril = row >= col
    cs = _dot3_left(tril.astype(BF16), adt)
    cs_t = cs.T
    cs_last = cs[Q - 1:Q, :]

    eh = lax.broadcasted_iota(jnp.int32, (LANES, SSM_D_INNER), 0)
    ec = lax.broadcasted_iota(jnp.int32, (LANES, SSM_D_INNER), 1)
    expand = (ec // SSM_HEAD_DIM == eh).astype(BF16)
    xdt = xs * _dot3(dt, expand)
    dec_out = _dot3(jnp.exp(cs), expand)
    dec_end = _dot3(jnp.exp(cs_last - cs), expand)
    dec_chunk = _dot3(jnp.broadcast_to(jnp.exp(cs_last), (8, LANES)), expand)[0:1, :]

    xdt_b = xdt.astype(BF16)
    xend_b = (xdt * dec_end).astype(BF16)
    lane = lax.broadcasted_iota(jnp.int32, (Q, LANES), 1)
    low_half = lane < SSM_HEAD_DIM

    for g in range(SSM_GROUPS):
        bg = xbc[:, SSM_D_INNER + g * SSM_D_STATE:SSM_D_INNER + (g + 1) * SSM_D_STATE]
        cg = xbc[:, SSM_D_INNER + SSM_BC_DIM + g * SSM_D_STATE:
                 SSM_D_INNER + SSM_BC_DIM + (g + 1) * SSM_D_STATE]
        bg_b = bg.astype(BF16)
        cg_b = cg.astype(BF16)
        cb = lax.dot_general(cg_b, bg_b, (((1,), (1,)), ((), ())), preferred_element_type=F32)
        for pr in range(SSM_HEADS_PER_GROUP // 2):
            pair = g * (SSM_HEADS_PER_GROUP // 2) + pr
            xp = xdt_b[:, pair * LANES:(pair + 1) * LANES]
            acc = None
            for e in range(2):
                h = 2 * pair + e
                seg = cs[:, h:h + 1] - cs_t[h:h + 1, :]
                m = (cb * jnp.exp(jnp.where(tril, seg, NEG_INF))).astype(BF16)
                xm = jnp.where(low_half if e == 0 else jnp.logical_not(low_half), xp, jnp.zeros_like(xp))
                t = jnp.dot(m, xm, preferred_element_type=F32)
                acc = t if acc is None else acc + t
            ydiag_ref[:, pair * LANES:(pair + 1) * LANES] = acc
        gs = slice(g * SSM_GROUP_WIDTH, (g + 1) * SSM_GROUP_WIDTH)
        st = state_ref[g]
        y_off = jnp.dot(cg_b, st.astype(BF16), preferred_element_type=F32) * dec_out[:, gs]
        ydiag_ref[:, gs] = ydiag_ref[:, gs] + y_off
        upd = jnp.dot(bg.T.astype(BF16), xend_b[:, gs], preferred_element_type=F32)
        state_ref[g] = st * dec_chunk[:, gs] + upd

    y = ydiag_ref[...] + xs * dskip_ref[...]
    y = y * _silu(z_ref[0])
    for g in range(SSM_GROUPS):
        gs = slice(g * SSM_GROUP_WIDTH, (g + 1) * SSM_GROUP_WIDTH)
        yg = y[:, gs]
        yg = yg * lax.rsqrt(jnp.mean(yg * yg, -1, keepdims=True) + LN_EPS)
        y_ref[0, :, gs] = (yg * nw_ref[:, gs]).astype(y_ref.dtype)


def _ssd_core(z, xbc, dt, conv_w, conv_b, dt_bias, a_log, d_skip, norm_w):
    B, S, _ = z.shape
    Q = SSM_CHUNK
    nc = S // Q
    blk = lambda w: pl.BlockSpec((1, Q, w), lambda b, c: (b, c, 0))
    return pl.pallas_call(
        _ssd_kernel,
        grid=(B, nc),
        in_specs=[
            blk(SSM_D_INNER), blk(SSM_CONV_DIM), blk(LANES),
            _const_spec((SSM_CONV, SSM_CONV_DIM)), _const_spec((1, SSM_CONV_DIM)),
            _const_spec((1, LANES)), _const_spec((1, LANES)),
            _const_spec((1, SSM_D_INNER)), _const_spec((1, SSM_D_INNER)),
        ],
        out_specs=blk(SSM_D_INNER),
        out_shape=jax.ShapeDtypeStruct((B, S, SSM_D_INNER), BF16),
        scratch_shapes=[
            pltpu.VMEM((8 + Q, SSM_CONV_DIM), F32),
            pltpu.VMEM((SSM_GROUPS, SSM_D_STATE, SSM_GROUP_WIDTH), F32),
            pltpu.VMEM((Q, SSM_D_INNER), F32),
        ],
        compiler_params=_params("arbitrary", "arbitrary"),
        name="ssd_core",
    )(z, xbc, dt, conv_w, conv_b.reshape(1, -1), _pad_lanes(dt_bias.reshape(1, -1)),
      _pad_lanes(a_log.reshape(1, -1)),
      jnp.repeat(d_skip, SSM_HEAD_DIM).reshape(1, -1), norm_w.reshape(1, -1))


def _mamba2_mixer(h, B, S, w_in, conv_w, conv_b, dt_bias, a_log, d_skip, norm_w, w_out, ln_g, ln_b):
    wz = w_in[:, :SSM_D_INNER].astype(BF16)
    wx = w_in[:, SSM_D_INNER:SSM_D_INNER + SSM_CONV_DIM].astype(BF16)
    wdt = _pad_lanes(w_in[:, SSM_D_INNER + SSM_CONV_DIM:]).astype(BF16)
    z, xbc, dt = _ssd_proj(h, wz, wx, wdt)
    y = _ssd_core(z.reshape(B, S, -1), xbc.reshape(B, S, -1), dt.reshape(B, S, -1),
                  conv_w, conv_b, dt_bias, a_log, d_skip, norm_w)
    return _outproj_ln(y.reshape(B * S, -1), w_out.astype(BF16), h, ln_g, ln_b)


def _head_masks(n_rows, width):
    lane = lax.broadcasted_iota(jnp.int32, (n_rows, width), 1)
    low = (lane & ATT_HEAD_DIM) == 0
    return [low, jnp.logical_not(low)]


LOG2E = 1.4426950408889634


def _pv_update(p_ref, a_ref, vt, acc_ref, tq):
    d = ATT_HEAD_DIM
    for e in range(2):
        rows = slice(e * d, (e + 1) * d)
        cols = slice(e * tq, (e + 1) * tq)
        acc_ref[rows, cols] = a_ref[:, cols] * acc_ref[rows, cols] + jnp.dot(
            vt[rows], p_ref[:, cols], preferred_element_type=F32)


def _qk_softmax(k2, qcat, adjust, m_ref, l_ref, p_ref, a_ref, tq, g, first=False, pending=None):
    n_groups = 2 * tq // g

    def qk(c):
        return lax.dot_general(k2, qcat[c * g:(c + 1) * g], (((1,), (1,)), ((), ())),
                               preferred_element_type=F32)

    s_next = qk(0)
    if pending is not None:
        pending()
    for c in range(n_groups):
        s = s_next
        if c + 1 < n_groups:
            s_next = qk(c + 1)
        s = adjust(c, s)
        cols = slice(c * g, (c + 1) * g)
        s_max = jnp.max(s, 0, keepdims=True)
        if first:
            m_new = s_max
            p = jnp.exp2(s - m_new)
            l_ref[:, cols] = jnp.sum(p, 0, keepdims=True)
            a_ref[:, cols] = jnp.ones_like(m_new)
        else:
            m_old = m_ref[:, cols]
            m_new = jnp.maximum(m_old, s_max)
            a = jnp.exp2(m_old - m_new)
            p = jnp.exp2(s - m_new)
            l_ref[:, cols] = a * l_ref[:, cols] + jnp.sum(p, 0, keepdims=True)
            a_ref[:, cols] = a
        p_ref[:, cols] = p.astype(BF16)
        m_ref[:, cols] = m_new


def _past_tiles_in_pairs(n, past_tile):
    odd = n % 2

    @pl.when(odd == 1)
    def _():
        past_tile(0, n)

    def body(i, prev):
        j = odd + 2 * i
        past_tile(j, prev)
        past_tile(j + 1, j)
        return j + 1

    return lax.fori_loop(0, n // 2, body, jnp.where(odd == 1, 0, n))


def _finish_pair(o_ref, l_ref, acc_ref, tq):
    d = ATT_HEAD_DIM
    ot = jnp.concatenate([acc_ref[0:d, 0:tq] / l_ref[:, 0:tq],
                          acc_ref[d:2 * d, tq:2 * tq] / l_ref[:, tq:2 * tq]], axis=0)
    o_ref[0] = ot.T.astype(o_ref.dtype)


def _fox_proj_kernel(x_ref, wq_ref, wk_ref, wvt_ref, wf_ref, q_ref, k_ref, vt_ref, f_ref):
    xb = x_ref[...].astype(BF16)
    q = jnp.dot(xb, wq_ref[...], preferred_element_type=F32)
    q_ref[...] = (q * (ATT_SCALE * LOG2E)).astype(BF16)
    k_ref[...] = jnp.dot(xb, wk_ref[...], preferred_element_type=F32).astype(BF16)
    vt_ref[0] = lax.dot_general(wvt_ref[...], xb, (((1,), (1,)), ((), ())),
                                preferred_element_type=F32).astype(BF16)
    f_ref[...] = jnp.dot(xb, wf_ref[...], preferred_element_type=F32)


def _fox_proj(h, B, S, wq, wk, wvt, wf, tm=512):
    T, D = h.shape
    HD = wq.shape[1]
    spt = S // tm
    return pl.pallas_call(
        _fox_proj_kernel,
        grid=(T // tm,),
        in_specs=[pl.BlockSpec((tm, D), lambda i: (i, 0)),
                  _const_spec((D, HD)), _const_spec((D, HD)), _const_spec((HD, D)),
                  _const_spec((D, LANES))],
        out_specs=[pl.BlockSpec((tm, HD), lambda i: (i, 0)),
                   pl.BlockSpec((tm, HD), lambda i: (i, 0)),
                   pl.BlockSpec((1, HD, tm), lambda i: (i // spt, 0, i % spt)),
                   pl.BlockSpec((tm, LANES), lambda i: (i, 0))],
        out_shape=[jax.ShapeDtypeStruct((T, HD), BF16), jax.ShapeDtypeStruct((T, HD), BF16),
                   jax.ShapeDtypeStruct((B, HD, S), BF16), jax.ShapeDtypeStruct((T, LANES), F32)],
        compiler_params=_params("parallel"),
        name="fox_proj",
    )(h, wq, wk, wvt, wf)


FOX_AUG_K = 0
FOX_AUG_Q = 3


def _cum_kernel(f_ref, bf_ref, qa_ref, ka_ref, carry_ref):
    tc = f_ref.shape[1]
    W = qa_ref.shape[2]

    @pl.when(pl.program_id(1) == 0)
    def _():
        carry_ref[...] = jnp.zeros_like(carry_ref)

    x = f_ref[0] + bf_ref[...]
    log_f = -_softplus(-x)
    row = lax.broadcasted_iota(jnp.int32, (tc, tc), 0)
    col = lax.broadcasted_iota(jnp.int32, (tc, tc), 1)
    cum = _dot3_left((row >= col).astype(BF16), log_f) + carry_ref[0:1, :]
    carry_ref[...] = jnp.broadcast_to(cum[tc - 1:tc, :], carry_ref.shape)

    hh = lax.broadcasted_iota(jnp.int32, (LANES, W), 0)
    cc = lax.broadcasted_iota(jnp.int32, (LANES, W), 1)
    base = (hh >> 1) * LANES + (hh & 1) * ATT_HEAD_DIM
    valid = hh < ATT_HEADS
    within = lax.broadcasted_iota(jnp.int32, (tc, W), 1) & (ATT_HEAD_DIM - 1)
    parts = _split3(cum * LOG2E)
    qa = jnp.where(jnp.logical_and(within >= FOX_AUG_K, within < FOX_AUG_K + 3), -1.0, 0.0)
    ka = jnp.where(jnp.logical_and(within >= FOX_AUG_Q, within < FOX_AUG_Q + 3), 1.0, 0.0)
    for c, part in enumerate(parts):
        pk = jnp.logical_and(valid, cc == base + (FOX_AUG_K + c)).astype(BF16)
        pq = jnp.logical_and(valid, cc == base + (FOX_AUG_Q + c)).astype(BF16)
        ka = ka + jnp.dot(part, pk, preferred_element_type=F32)
        qa = qa + jnp.dot(part, pq, preferred_element_type=F32)
    qa_ref[0] = qa.astype(BF16)
    ka_ref[0] = ka.astype(BF16)


def _fox_cum(f, b_f, width, tc=256):
    B, S, W = f.shape
    return pl.pallas_call(
        _cum_kernel,
        grid=(B, S // tc),
        in_specs=[pl.BlockSpec((1, tc, W), lambda b, c: (b, c, 0)), _const_spec((1, W))],
        out_specs=[pl.BlockSpec((1, tc, width), lambda b, c: (b, c, 0)),
                   pl.BlockSpec((1, tc, width), lambda b, c: (b, c, 0))],
        out_shape=[jax.ShapeDtypeStruct((B, S, width), BF16), jax.ShapeDtypeStruct((B, S, width), BF16)],
        scratch_shapes=[pltpu.VMEM((8, W), F32)],
        compiler_params=_params("arbitrary", "arbitrary"),
        name="fox_cum",
    )(f, _pad_lanes(b_f.reshape(1, -1), W))


def _fox_kernel(q_ref, qa_ref, k_ref, ka_ref, vt_ref, o_ref, m_ref, l_ref, acc_ref, p_ref, a_ref, *, t, g):
    qi = pl.program_id(2)
    q2 = jnp.concatenate([q_ref[0], qa_ref[0]], axis=1)
    masks = _head_masks(t, 2 * LANES)
    qcat = jnp.concatenate([jnp.where(masks[e], q2, jnp.zeros_like(q2)) for e in range(2)], axis=0)
    kpos = lax.broadcasted_iota(jnp.int32, (t, g), 0)
    qcol = lax.broadcasted_iota(jnp.int32, (t, g), 1)

    def causal_mask(c, s):
        return jnp.where(kpos <= ((qcol + c * g) & (t - 1)), s, NEG_INF)

    def keys(j):
        ks = pl.multiple_of(j * t, t)
        return jnp.concatenate([k_ref[0, pl.ds(ks, t), :], ka_ref[0, pl.ds(ks, t), :]], axis=1)

    def pv(j):
        _pv_update(p_ref, a_ref, vt_ref[0, :, pl.ds(pl.multiple_of(j * t, t), t)], acc_ref, t)

    acc_ref[...] = jnp.zeros_like(acc_ref)
    _qk_softmax(keys(qi), qcat, causal_mask, m_ref, l_ref, p_ref, a_ref, t, g, first=True)

    def past_tile(j, prev):
        _qk_softmax(keys(j), qcat, lambda c, s: s, m_ref, l_ref, p_ref, a_ref, t, g,
                    pending=lambda: pv(prev))

    last = _past_tiles_in_pairs(qi, past_tile)
    pv(last)
    _finish_pair(o_ref, l_ref, acc_ref, t)


def _fox_attention_core(q, qa, k, ka, vt, t=512, g=512):
    B, S, D = q.shape
    kern = functools.partial(_fox_kernel, t=t, g=g)
    tile = pl.BlockSpec((1, t, LANES), lambda b, hp, i: (b, i, hp))
    full = pl.BlockSpec((1, S, LANES), lambda b, hp, i: (b, 0, hp))
    return pl.pallas_call(
        kern,
        grid=(B, D // LANES, S // t),
        in_specs=[tile, tile, full, full,
                  pl.BlockSpec((1, LANES, S), lambda b, hp, i: (b, hp, 0))],
        out_specs=tile,
        out_shape=jax.ShapeDtypeStruct((B, S, D), BF16),
        scratch_shapes=[
            pltpu.VMEM((1, 2 * t), F32), pltpu.VMEM((1, 2 * t), F32),
            pltpu.VMEM((LANES, 2 * t), F32),
            pltpu.VMEM((t, 2 * t), BF16), pltpu.VMEM((1, 2 * t), F32),
        ],
        compiler_params=_params("parallel", "parallel", "arbitrary"),
        name="fox_attention",
    )(q, qa, k, ka, vt)


def _fox_mixer(h, B, S, w_in, b_f, w_out, ln_g, ln_b):
    HD = ATT_HEADS * ATT_HEAD_DIM
    wq = w_in[:, :HD].astype(BF16)
    wk = w_in[:, HD:2 * HD].astype(BF16)
    wvt = w_in[:, 2 * HD:3 * HD].T.astype(BF16)
    wf = _pad_lanes(w_in[:, 3 * HD:]).astype(BF16)
    q, k, vt, f = _fox_proj(h, B, S, wq, wk, wvt, wf)
    qa, ka = _fox_cum(f.reshape(B, S, LANES), b_f, HD)
    o = _fox_attention_core(q.reshape(B, S, HD), qa, k.reshape(B, S, HD), ka, vt)
    return _outproj_ln(o.reshape(B * S, HD), w_out.astype(BF16), h, ln_g, ln_b)


def _moba_proj_kernel(x_ref, wq_ref, wk_ref, wvt_ref, q_ref, k_ref, vt_ref):
    xb = x_ref[...].astype(BF16)
    q_ref[...] = jnp.dot(xb, wq_ref[...], preferred_element_type=F32)
    k_ref[...] = jnp.dot(xb, wk_ref[...], preferred_element_type=F32).astype(BF16)
    vt_ref[0] = lax.dot_general(wvt_ref[...], xb, (((1,), (1,)), ((), ())),
                                preferred_element_type=F32).astype(BF16)


def _moba_proj(h, B, S, wq, wk, wvt, tm=512):
    T, D = h.shape
    HD = wq.shape[1]
    spt = S // tm
    return pl.pallas_call(
        _moba_proj_kernel,
        grid=(T // tm,),
        in_specs=[pl.BlockSpec((tm, D), lambda i: (i, 0)),
                  _const_spec((D, HD)), _const_spec((D, HD)), _const_spec((HD, D))],
        out_specs=[pl.BlockSpec((tm, HD), lambda i: (i, 0)),
                   pl.BlockSpec((tm, HD), lambda i: (i, 0)),
                   pl.BlockSpec((1, HD, tm), lambda i: (i // spt, 0, i % spt))],
        out_shape=[jax.ShapeDtypeStruct((T, HD), F32), jax.ShapeDtypeStruct((T, HD), BF16),
                   jax.ShapeDtypeStruct((B, HD, S), BF16)],
        compiler_params=_params("parallel"),
        name="moba_proj",
    )(h, wq, wk, wvt)


def _moba_kernel(qf_ref, k_ref, vt_ref, o_ref, m_ref, l_ref, acc_ref, p_ref, a_ref, km_ref, bias_ref,
                 *, nb, g):
    L = MOBA_BLOCK
    qt = pl.program_id(2)

    @pl.when(qt == 0)
    def _():
        for n in range(nb):
            kb = k_ref[0, n * L:(n + 1) * L, :].astype(F32)
            km_ref[n:n + 1, :] = jnp.sum(kb, 0, keepdims=True) * (1.0 / L)

    t = 2 * L
    masks = _head_masks(t, LANES)
    qf = qf_ref[0]
    qfcat = jnp.concatenate([jnp.where(masks[e], qf, jnp.zeros_like(qf)) for e in range(2)], axis=0)
    qcat = (qfcat * (ATT_SCALE * LOG2E)).astype(BF16)
    colq = lax.broadcasted_iota(jnp.int32, (1, 2 * t), 1) & (t - 1)
    second = colq >= L
    own = (2 * qt + second.astype(jnp.int32)).astype(F32)
    blk = lax.broadcasted_iota(jnp.int32, (nb, 2 * t), 0).astype(F32)

    gate = lax.dot_general(km_ref[...], qfcat, (((1,), (1,)), ((), ())), preferred_element_type=F32,
                           precision=lax.Precision.HIGHEST)
    gate = jnp.where(blk < own, gate, NEG_INF)
    bias = jnp.full((nb, 2 * t), NEG_INF, F32)
    for _ in range(MOBA_TOPK):
        mx = jnp.max(gate, 0, keepdims=True)
        first = jnp.min(jnp.where(gate == mx, blk, float(nb)), 0, keepdims=True)
        pick = jnp.logical_and(blk == first, mx > NEG_INF)
        bias = jnp.where(pick, 0.0, bias)
        gate = jnp.where(pick, NEG_INF, gate)
    bias_ref[...] = bias

    kpos = lax.broadcasted_iota(jnp.int32, (t, g), 0)
    qcol = lax.broadcasted_iota(jnp.int32, (t, g), 1)

    def own_adjust(c, s):
        s = jnp.where(kpos <= ((qcol + c * g) & (t - 1)), s, NEG_INF)
        past = jnp.where(second[:, c * g:(c + 1) * g], bias_ref[pl.ds(2 * qt, 1), c * g:(c + 1) * g], 0.0)
        return jnp.concatenate([s[0:L] + past, s[L:t]], axis=0)

    def keys(j):
        return k_ref[0, pl.ds(pl.multiple_of(j * t, t), t), :]

    def pv(j):
        _pv_update(p_ref, a_ref, vt_ref[0, :, pl.ds(pl.multiple_of(j * t, t), t)], acc_ref, t)

    acc_ref[...] = jnp.zeros_like(acc_ref)
    _qk_softmax(keys(qt), qcat, own_adjust, m_ref, l_ref, p_ref, a_ref, t, g, first=True)

    def past_tile(j, prev):
        def past_adjust(c, s):
            b0 = bias_ref[pl.ds(2 * j, 1), c * g:(c + 1) * g]
            b1 = bias_ref[pl.ds(2 * j + 1, 1), c * g:(c + 1) * g]
            return jnp.concatenate([s[0:L] + b0, s[L:t] + b1], axis=0)

        _qk_softmax(keys(j), qcat, past_adjust, m_ref, l_ref, p_ref, a_ref, t, g,
                    pending=lambda: pv(prev))

    last = _past_tiles_in_pairs(qt, past_tile)
    pv(last)
    _finish_pair(o_ref, l_ref, acc_ref, t)


def _moba_attention_core(qf, k, vt, g=512):
    B, S, D = qf.shape
    t = 2 * MOBA_BLOCK
    nb = S // MOBA_BLOCK
    tile = pl.BlockSpec((1, t, LANES), lambda b, hp, i: (b, i, hp))
    return pl.pallas_call(
        functools.partial(_moba_kernel, nb=nb, g=g),
        grid=(B, D // LANES, S // t),
        in_specs=[tile,
                  pl.BlockSpec((1, S, LANES), lambda b, hp, i: (b, 0, hp)),
                  pl.BlockSpec((1, LANES, S), lambda b, hp, i: (b, hp, 0))],
        out_specs=tile,
        out_shape=jax.ShapeDtypeStruct((B, S, D), BF16),
        scratch_shapes=[
            pltpu.VMEM((1, 2 * t), F32), pltpu.VMEM((1, 2 * t), F32),
            pltpu.VMEM((LANES, 2 * t), F32),
            pltpu.VMEM((t, 2 * t), BF16), pltpu.VMEM((1, 2 * t), F32),
            pltpu.VMEM((nb, LANES), F32), pltpu.VMEM((nb, 2 * t), F32),
        ],
        compiler_params=_params("parallel", "arbitrary", "arbitrary"),
        name="moba_attention",
    )(qf, k, vt)


def _moba_mixer(h, B, S, w_in, w_out, ln_g, ln_b):
    HD = ATT_HEADS * ATT_HEAD_DIM
    wq = w_in[:, :HD].astype(BF16)
    wk = w_in[:, HD:2 * HD].astype(BF16)
    wvt = w_in[:, 2 * HD:3 * HD].T.astype(BF16)
    qf, k, vt = _moba_proj(h, B, S, wq, wk, wvt)
    o = _moba_attention_core(qf.reshape(B, S, HD), k.reshape(B, S, HD), vt)
    return _outproj_ln(o.reshape(B * S, HD), w_out.astype(BF16), h, ln_g, ln_b)


def kernel(x, ffn_w_gate, ffn_w_up, ffn_w_down, ln_g, ln_b, ssm_w_in, ssm_conv_w, ssm_conv_b,
           ssm_dt_bias, ssm_a_log, ssm_d, ssm_norm_w, ssm_w_out, fox_w_in, fox_b_f, fox_w_out,
           moba_w_in, moba_w_out):
    B, S, D = x.shape
    h = x.reshape(B * S, D)
    for layer in range(DEPTH):
        kind, j = layer % 3, layer // 3
        h = _ffn_ln(h, ffn_w_gate[layer, 0].astype(BF16), ffn_w_up[layer, 0].astype(BF16),
                    ffn_w_down[layer, 0].astype(BF16), ln_g[layer, 0], ln_b[layer, 0])
        if kind == 0:
            h = _mamba2_mixer(h, B, S, ssm_w_in[j], ssm_conv_w[j], ssm_conv_b[j], ssm_dt_bias[j],
                              ssm_a_log[j], ssm_d[j], ssm_norm_w[j], ssm_w_out[j],
                              ln_g[layer, 1], ln_b[layer, 1])
        elif kind == 1:
            h = _fox_mixer(h, B, S, fox_w_in[j], fox_b_f[j], fox_w_out[j],
                           ln_g[layer, 1], ln_b[layer, 1])
        else:
            h = _moba_mixer(h, B, S, moba_w_in[j], moba_w_out[j], ln_g[layer, 1], ln_b[layer, 1])
        h = _ffn_ln(h, ffn_w_gate[layer, 1].astype(BF16), ffn_w_up[layer, 1].astype(BF16),
                    ffn_w_down[layer, 1].astype(BF16), ln_g[layer, 2], ln_b[layer, 2])
    return h.reshape(B, S, D)
```

```python
import functools

import jax
import jax.numpy as jnp
from jax import lax
from jax.experimental import pallas as pl
from jax.experimental.pallas import tpu as pltpu

F32 = jnp.float32
BF16 = jnp.bfloat16

DEPTH = 4
D_MODEL = 1024
D_FF = 2816
LN_EPS = 1e-5
DEEPNORM_ALPHA = (2 * DEPTH) ** 0.25
MACARON_WEIGHT = 0.5

SSM_D_INNER = 2048
SSM_HEAD_DIM = 64
SSM_HEADS = 32
SSM_GROUPS = 4
SSM_HEADS_PER_GROUP = 8
SSM_D_STATE = 128
SSM_CONV = 4
SSM_CHUNK = 256
SSM_BC_DIM = SSM_GROUPS * SSM_D_STATE
SSM_CONV_DIM = SSM_D_INNER + 2 * SSM_BC_DIM
SSM_GROUP_WIDTH = SSM_D_INNER // SSM_GROUPS

ATT_HEAD_DIM = 64
ATT_HEADS = 16
ATT_SCALE = ATT_HEAD_DIM ** -0.5
MOBA_BLOCK = 256
MOBA_TOPK = 3

LANES = 128
NEG_INF = float("-inf")
VMEM_LIMIT = 56 * 1024 * 1024


def _params(*sem):
    return pltpu.CompilerParams(dimension_semantics=sem, vmem_limit_bytes=VMEM_LIMIT)


def _const_spec(shape):
    nd = len(shape)
    return pl.BlockSpec(shape, lambda *_: (0,) * nd, pipeline_mode=pl.Buffered(1))


def _layer_norm(v, g, b):
    mu = jnp.mean(v, -1, keepdims=True)
    d = v - mu
    var = jnp.mean(d * d, -1, keepdims=True)
    return d * lax.rsqrt(var + LN_EPS) * g + b


def _split3(v):
    hi = v.astype(BF16)
    r1 = v - hi.astype(F32)
    mid = r1.astype(BF16)
    lo = (r1 - mid.astype(F32)).astype(BF16)
    return hi, mid, lo


def _dot3(v, m01):
    out = None
    for part in _split3(v):
        t = jnp.dot(part, m01, preferred_element_type=F32)
        out = t if out is None else out + t
    return out


def _dot3_left(m01, v):
    out = None
    for part in _split3(v):
        t = jnp.dot(m01, part, preferred_element_type=F32)
        out = t if out is None else out + t
    return out


def _softplus(x):
    return jnp.maximum(x, 0.0) + jnp.log1p(jnp.exp(-jnp.abs(x)))


def _silu(x):
    return x * jax.nn.sigmoid(x)


def _pad_lanes(v, n=LANES):
    return jnp.pad(v, [(0, 0)] * (v.ndim - 1) + [(0, n - v.shape[-1])])


def _ffn_kernel(h_ref, wg_ref, wu_ref, wd_ref, g_ref, b_ref, o_ref):
    h = h_ref[...]
    hb = h.astype(BF16)
    gate = jnp.dot(hb, wg_ref[...], preferred_element_type=F32)
    up = jnp.dot(hb, wu_ref[...], preferred_element_type=F32)
    act = (_silu(gate) * up).astype(BF16)
    ff = jnp.dot(act, wd_ref[...], preferred_element_type=F32)
    o_ref[...] = _layer_norm(DEEPNORM_ALPHA * h + MACARON_WEIGHT * ff, g_ref[...], b_ref[...])


def _ffn_ln(h, wg, wu, wd, g, b, tm=512):
    T, D = h.shape
    FF = wg.shape[1]
    return pl.pallas_call(
        _ffn_kernel,
        grid=(T // tm,),
        in_specs=[
            pl.BlockSpec((tm, D), lambda i: (i, 0)),
            _const_spec((D, FF)), _const_spec((D, FF)), _const_spec((FF, D)),
            _const_spec((1, D)), _const_spec((1, D)),
        ],
        out_specs=pl.BlockSpec((tm, D), lambda i: (i, 0)),
        out_shape=jax.ShapeDtypeStruct((T, D), F32),
        compiler_params=_params("parallel"),
        name="ffn_ln",
    )(h, wg, wu, wd, g.reshape(1, D), b.reshape(1, D))


def _ssd_proj_kernel(x_ref, wz_ref, wx_ref, wdt_ref, z_ref, xbc_ref, dt_ref):
    xb = x_ref[...].astype(BF16)
    z_ref[...] = jnp.dot(xb, wz_ref[...], preferred_element_type=F32)
    xbc_ref[...] = jnp.dot(xb, wx_ref[...], preferred_element_type=F32)
    dt_ref[...] = jnp.dot(xb, wdt_ref[...], preferred_element_type=F32)


def _ssd_proj(x, wz, wx, wdt, tm=512):
    T, D = x.shape
    widths = (wz.shape[1], wx.shape[1], wdt.shape[1])
    return pl.pallas_call(
        _ssd_proj_kernel,
        grid=(T // tm,),
        in_specs=[pl.BlockSpec((tm, D), lambda i: (i, 0))] + [_const_spec((D, n)) for n in widths],
        out_specs=[pl.BlockSpec((tm, n), lambda i: (i, 0)) for n in widths],
        out_shape=[jax.ShapeDtypeStruct((T, n), F32) for n in widths],
        compiler_params=_params("parallel"),
        name="ssd_proj",
    )(x, wz, wx, wdt)


def _outproj_kernel(y_ref, w_ref, h_ref, g_ref, b_ref, o_ref):
    mix = jnp.dot(y_ref[...].astype(BF16), w_ref[...], preferred_element_type=F32)
    o_ref[...] = _layer_norm(DEEPNORM_ALPHA * h_ref[...] + mix, g_ref[...], b_ref[...])


def _outproj_ln(y, w, h, g, b, tm=512):
    T, K = y.shape
    D = w.shape[1]
    return pl.pallas_call(
        _outproj_kernel,
        grid=(T // tm,),
        in_specs=[
            pl.BlockSpec((tm, K), lambda i: (i, 0)), _const_spec((K, D)),
            pl.BlockSpec((tm, D), lambda i: (i, 0)),
            _const_spec((1, D)), _const_spec((1, D)),
        ],
        out_specs=pl.BlockSpec((tm, D), lambda i: (i, 0)),
        out_shape=jax.ShapeDtypeStruct((T, D), F32),
        compiler_params=_params("parallel"),
        name="outproj_ln",
    )(y, w, h, g.reshape(1, D), b.reshape(1, D))


def _ssd_kernel(z_ref, xbc_ref, dt_ref, cw_ref, cb_ref, dtb_ref, alog_ref, dskip_ref, nw_ref,
                y_ref, pad_ref, state_ref, ydiag_ref):
    Q = SSM_CHUNK
    c = pl.program_id(1)

    @pl.when(c == 0)
    def _():
        pad_ref[0:8, :] = jnp.zeros((8, SSM_CONV_DIM), F32)
        state_ref[...] = jnp.zeros_like(state_ref)

    xraw = xbc_ref[0]
    pad_ref[8:8 + Q, :] = xraw
    conv = cb_ref[...] + cw_ref[SSM_CONV - 1:SSM_CONV, :] * xraw
    for k in range(1, SSM_CONV):
        conv = conv + cw_ref[SSM_CONV - 1 - k:SSM_CONV - k, :] * pad_ref[pl.ds(8 - k, Q), :]
    pad_ref[0:8, :] = xraw[Q - 8:Q, :]
    xbc = _silu(conv)
    xs = xbc[:, :SSM_D_INNER]

    dt = _softplus(dt_ref[0] + dtb_ref[...])
    adt = dt * (-jnp.exp(alog_ref[...]))
    row = lax.broadcasted_iota(jnp.int32, (Q, Q), 0)
    col = lax.broadcasted_iota(jnp.int32, (Q, Q), 1)
    tril = row >= col
    cs = _dot3_left(tril.astype(BF16), adt)
    cs_t = cs.T
    cs_last = cs[Q - 1:Q, :]

    eh = lax.broadcasted_iota(jnp.int32, (LANES, SSM_D_INNER), 0)
    ec = lax.broadcasted_iota(jnp.int32, (LANES, SSM_D_INNER), 1)
    expand = (ec // SSM_HEAD_DIM == eh).astype(BF16)
    xdt = xs * _dot3(dt, expand)
    dec_out = _dot3(jnp.exp(cs), expand)
    dec_end = _dot3(jnp.exp(cs_last - cs), expand)
    dec_chunk = _dot3(jnp.broadcast_to(jnp.exp(cs_last), (8, LANES)), expand)[0:1, :]

    xdt_b = xdt.astype(BF16)
    xend_b = (xdt * dec_end).astype(BF16)
    lane = lax.broadcasted_iota(jnp.int32, (Q, LANES), 1)
    low_half = lane < SSM_HEAD_DIM

    for g in range(SSM_GROUPS):
        bg = xbc[:, SSM_D_INNER + g * SSM_D_STATE:SSM_D_INNER + (g + 1) * SSM_D_STATE]
        cg = xbc[:, SSM_D_INNER + SSM_BC_DIM + g * SSM_D_STATE:
                 SSM_D_INNER + SSM_BC_DIM + (g + 1) * SSM_D_STATE]
        bg_b = bg.astype(BF16)
        cg_b = cg.astype(BF16)
        cb = lax.dot_general(cg_b, bg_b, (((1,), (1,)), ((), ())), preferred_element_type=F32)
        for pr in range(SSM_HEADS_PER_GROUP // 2):
            pair = g * (SSM_HEADS_PER_GROUP // 2) + pr
            xp = xdt_b[:, pair * LANES:(pair + 1) * LANES]
            acc = None
            for e in range(2):
                h = 2 * pair + e
                seg = cs[:, h:h + 1] - cs_t[h:h + 1, :]
                m = (cb * jnp.exp(jnp.where(tril, seg, NEG_INF))).astype(BF16)
                xm = jnp.where(low_half if e == 0 else jnp.logical_not(low_half), xp, jnp.zeros_like(xp))
                t = jnp.dot(m, xm, preferred_element_type=F32)
                acc = t if acc is None else acc + t
            ydiag_ref[:, pair * LANES:(pair + 1) * LANES] = acc
        gs = slice(g * SSM_GROUP_WIDTH, (g + 1) * SSM_GROUP_WIDTH)
        st = state_ref[g]
        y_off = jnp.dot(cg_b, st.astype(BF16), preferred_element_type=F32) * dec_out[:, gs]
        ydiag_ref[:, gs] = ydiag_ref[:, gs] + y_off
        upd = jnp.dot(bg.T.astype(BF16), xend_b[:, gs], preferred_element_type=F32)
        state_ref[g] = st * dec_chunk[:, gs] + upd

    y = ydiag_ref[...] + xs * dskip_ref[...]
    y = y * _silu(z_ref[0])
    for g in range(SSM_GROUPS):
        gs = slice(g * SSM_GROUP_WIDTH, (g + 1) * SSM_GROUP_WIDTH)
        yg = y[:, gs]
        yg = yg * lax.rsqrt(jnp.mean(yg * yg, -1, keepdims=True) + LN_EPS)
        y_ref[0, :, gs] = (yg * nw_ref[:, gs]).astype(y_ref.dtype)


def _ssd_core(z, xbc, dt, conv_w, conv_b, dt_bias, a_log, d_skip, norm_w):
    B, S, _ = z.shape
    Q = SSM_CHUNK
    nc = S // Q
    blk = lambda w: pl.BlockSpec((1, Q, w), lambda b, c: (b, c, 0))
    return pl.pallas_call(
        _ssd_kernel,
        grid=(B, nc),
        in_specs=[
            blk(SSM_D_INNER), blk(SSM_CONV_DIM), blk(LANES),
            _const_spec((SSM_CONV, SSM_CONV_DIM)), _const_spec((1, SSM_CONV_DIM)),
            _const_spec((1, LANES)), _const_spec((1, LANES)),
            _const_spec((1, SSM_D_INNER)), _const_spec((1, SSM_D_INNER)),
        ],
        out_specs=blk(SSM_D_INNER),
        out_shape=jax.ShapeDtypeStruct((B, S, SSM_D_INNER), BF16),
        scratch_shapes=[
            pltpu.VMEM((8 + Q, SSM_CONV_DIM), F32),
            pltpu.VMEM((SSM_GROUPS, SSM_D_STATE, SSM_GROUP_WIDTH), F32),
            pltpu.VMEM((Q, SSM_D_INNER), F32),
        ],
        compiler_params=_params("arbitrary", "arbitrary"),
        name="ssd_core",
    )(z, xbc, dt, conv_w, conv_b.reshape(1, -1), _pad_lanes(dt_bias.reshape(1, -1)),
      _pad_lanes(a_log.reshape(1, -1)),
      jnp.repeat(d_skip, SSM_HEAD_DIM).reshape(1, -1), norm_w.reshape(1, -1))


def _mamba2_mixer(h, B, S, w_in, conv_w, conv_b, dt_bias, a_log, d_skip, norm_w, w_out, ln_g, ln_b):
    wz = w_in[:, :SSM_D_INNER].astype(BF16)
    wx = w_in[:, SSM_D_INNER:SSM_D_INNER + SSM_CONV_DIM].astype(BF16)
    wdt = _pad_lanes(w_in[:, SSM_D_INNER + SSM_CONV_DIM:]).astype(BF16)
    z, xbc, dt = _ssd_proj(h, wz, wx, wdt)
    y = _ssd_core(z.reshape(B, S, -1), xbc.reshape(B, S, -1), dt.reshape(B, S, -1),
                  conv_w, conv_b, dt_bias, a_log, d_skip, norm_w)
    return _outproj_ln(y.reshape(B * S, -1), w_out.astype(BF16), h, ln_g, ln_b)


def _head_masks(n_rows, width):
    lane = lax.broadcasted_iota(jnp.int32, (n_rows, width), 1)
    low = (lane & ATT_HEAD_DIM) == 0
    return [low, jnp.logical_not(low)]


LOG2E = 1.4426950408889634


def _pv_update(p_ref, a_ref, vt, acc_ref, tq):
    d = ATT_HEAD_DIM
    for e in range(2):
        rows = slice(e * d, (e + 1) * d)
        cols = slice(e * tq, (e + 1) * tq)
        acc_ref[rows, cols] = a_ref[:, cols] * acc_ref[rows, cols] + jnp.dot(
            vt[rows], p_ref[:, cols], preferred_element_type=F32)


def _qk_softmax(k2, qcat, adjust, m_ref, l_ref, p_ref, a_ref, tq, g, first=False, pending=None):
    n_groups = 2 * tq // g

    def qk(c):
        return lax.dot_general(k2, qcat[c * g:(c + 1) * g], (((1,), (1,)), ((), ())),
                               preferred_element_type=F32)

    s_next = qk(0)
    if pending is not None:
        pending()
    for c in range(n_groups):
        s = s_next
        if c + 1 < n_groups:
            s_next = qk(c + 1)
        s = adjust(c, s)
        cols = slice(c * g, (c + 1) * g)
        s_max = jnp.max(s, 0, keepdims=True)
        if first:
            m_new = s_max
            p = jnp.exp2(s - m_new)
            l_ref[:, cols] = jnp.sum(p, 0, keepdims=True)
            a_ref[:, cols] = jnp.ones_like(m_new)
        else:
            m_old = m_ref[:, cols]
            m_new = jnp.maximum(m_old, s_max)
            a = jnp.exp2(m_old - m_new)
            p = jnp.exp2(s - m_new)
            l_ref[:, cols] = a * l_ref[:, cols] + jnp.sum(p, 0, keepdims=True)
            a_ref[:, cols] = a
        p_ref[:, cols] = p.astype(BF16)
        m_ref[:, cols] = m_new


def _past_tiles_in_pairs(n, past_tile):
    one = n & 1
    two = n & 2

    @pl.when(one != 0)
    def _():
        past_tile(0, n)

    prev1 = jnp.where(one != 0, 0, n)

    @pl.when(two != 0)
    def _():
        past_tile(one, prev1)
        past_tile(one + 1, one)

    def body(i, prev):
        j = one + two + 4 * i
        past_tile(j, prev)
        past_tile(j + 1, j)
        past_tile(j + 2, j + 1)
        past_tile(j + 3, j + 2)
        return j + 3

    return lax.fori_loop(0, n // 4, body, jnp.where(two != 0, one + 1, prev1))


def _finish_pair(o_ref, l_ref, acc_ref, tq):
    d = ATT_HEAD_DIM
    ot = jnp.concatenate([acc_ref[0:d, 0:tq] / l_ref[:, 0:tq],
                          acc_ref[d:2 * d, tq:2 * tq] / l_ref[:, tq:2 * tq]], axis=0)
    o_ref[0] = ot.T.astype(o_ref.dtype)


def _fox_proj_kernel(x_ref, wq_ref, wk_ref, wvt_ref, wf_ref, q_ref, k_ref, vt_ref, f_ref):
    xb = x_ref[...].astype(BF16)
    q = jnp.dot(xb, wq_ref[...], preferred_element_type=F32)
    q_ref[...] = (q * (ATT_SCALE * LOG2E)).astype(BF16)
    k_ref[...] = jnp.dot(xb, wk_ref[...], preferred_element_type=F32).astype(BF16)
    vt_ref[0] = lax.dot_general(wvt_ref[...], xb, (((1,), (1,)), ((), ())),
                                preferred_element_type=F32).astype(BF16)
    f_ref[...] = jnp.dot(xb, wf_ref[...], preferred_element_type=F32)


def _fox_proj(h, B, S, wq, wk, wvt, wf, tm=512):
    T, D = h.shape
    HD = wq.shape[1]
    spt = S // tm
    return pl.pallas_call(
        _fox_proj_kernel,
        grid=(T // tm,),
        in_specs=[pl.BlockSpec((tm, D), lambda i: (i, 0)),
                  _const_spec((D, HD)), _const_spec((D, HD)), _const_spec((HD, D)),
                  _const_spec((D, LANES))],
        out_specs=[pl.BlockSpec((tm, HD), lambda i: (i, 0)),
                   pl.BlockSpec((tm, HD), lambda i: (i, 0)),
                   pl.BlockSpec((1, HD, tm), lambda i: (i // spt, 0, i % spt)),
                   pl.BlockSpec((tm, LANES), lambda i: (i, 0))],
        out_shape=[jax.ShapeDtypeStruct((T, HD), BF16), jax.ShapeDtypeStruct((T, HD), BF16),
                   jax.ShapeDtypeStruct((B, HD, S), BF16), jax.ShapeDtypeStruct((T, LANES), F32)],
        compiler_params=_params("parallel"),
        name="fox_proj",
    )(h, wq, wk, wvt, wf)


FOX_AUG_K = 0
FOX_AUG_Q = 3


def _cum_kernel(f_ref, bf_ref, qa_ref, ka_ref, carry_ref):
    tc = f_ref.shape[1]
    W = qa_ref.shape[2]

    @pl.when(pl.program_id(1) == 0)
    def _():
        carry_ref[...] = jnp.zeros_like(carry_ref)

    x = f_ref[0] + bf_ref[...]
    log_f = -_softplus(-x)
    row = lax.broadcasted_iota(jnp.int32, (tc, tc), 0)
    col = lax.broadcasted_iota(jnp.int32, (tc, tc), 1)
    cum = _dot3_left((row >= col).astype(BF16), log_f) + carry_ref[0:1, :]
    carry_ref[...] = jnp.broadcast_to(cum[tc - 1:tc, :], carry_ref.shape)

    hh = lax.broadcasted_iota(jnp.int32, (LANES, W), 0)
    cc = lax.broadcasted_iota(jnp.int32, (LANES, W), 1)
    base = (hh >> 1) * LANES + (hh & 1) * ATT_HEAD_DIM
    valid = hh < ATT_HEADS
    within = lax.broadcasted_iota(jnp.int32, (tc, W), 1) & (ATT_HEAD_DIM - 1)
    parts = _split3(cum * LOG2E)
    qa = jnp.where(jnp.logical_and(within >= FOX_AUG_K, within < FOX_AUG_K + 3), -1.0, 0.0)
    ka = jnp.where(jnp.logical_and(within >= FOX_AUG_Q, within < FOX_AUG_Q + 3), 1.0, 0.0)
    for c, part in enumerate(parts):
        pk = jnp.logical_and(valid, cc == base + (FOX_AUG_K + c)).astype(BF16)
        pq = jnp.logical_and(valid, cc == base + (FOX_AUG_Q + c)).astype(BF16)
        ka = ka + jnp.dot(part, pk, preferred_element_type=F32)
        qa = qa + jnp.dot(part, pq, preferred_element_type=F32)
    qa_ref[0] = qa.astype(BF16)
    ka_ref[0] = ka.astype(BF16)


def _fox_cum(f, b_f, width, tc=256):
    B, S, W = f.shape
    return pl.pallas_call(
        _cum_kernel,
        grid=(B, S // tc),
        in_specs=[pl.BlockSpec((1, tc, W), lambda b, c: (b, c, 0)), _const_spec((1, W))],
        out_specs=[pl.BlockSpec((1, tc, width), lambda b, c: (b, c, 0)),
                   pl.BlockSpec((1, tc, width), lambda b, c: (b, c, 0))],
        out_shape=[jax.ShapeDtypeStruct((B, S, width), BF16), jax.ShapeDtypeStruct((B, S, width), BF16)],
        scratch_shapes=[pltpu.VMEM((8, W), F32)],
        compiler_params=_params("arbitrary", "arbitrary"),
        name="fox_cum",
    )(f, _pad_lanes(b_f.reshape(1, -1), W))


def _fox_kernel(q_ref, qa_ref, k_ref, ka_ref, vt_ref, o_ref, m_ref, l_ref, acc_ref, p_ref, a_ref, *, t, g):
    qi = pl.program_id(2)
    q2 = jnp.concatenate([q_ref[0], qa_ref[0]], axis=1)
    masks = _head_masks(t, 2 * LANES)
    qcat = jnp.concatenate([jnp.where(masks[e], q2, jnp.zeros_like(q2)) for e in range(2)], axis=0)
    kpos = lax.broadcasted_iota(jnp.int32, (t, g), 0)
    qcol = lax.broadcasted_iota(jnp.int32, (t, g), 1)

    def causal_mask(c, s):
        return jnp.where(kpos <= ((qcol + c * g) & (t - 1)), s, NEG_INF)

    def keys(j):
        ks = pl.multiple_of(j * t, t)
        return jnp.concatenate([k_ref[0, pl.ds(ks, t), :], ka_ref[0, pl.ds(ks, t), :]], axis=1)

    def pv(j):
        _pv_update(p_ref, a_ref, vt_ref[0, :, pl.ds(pl.multiple_of(j * t, t), t)], acc_ref, t)

    acc_ref[...] = jnp.zeros_like(acc_ref)
    _qk_softmax(keys(qi), qcat, causal_mask, m_ref, l_ref, p_ref, a_ref, t, g, first=True)

    def past_tile(j, prev):
        _qk_softmax(keys(j), qcat, lambda c, s: s, m_ref, l_ref, p_ref, a_ref, t, g,
                    pending=lambda: pv(prev))

    last = _past_tiles_in_pairs(qi, past_tile)
    pv(last)
    _finish_pair(o_ref, l_ref, acc_ref, t)


def _fox_attention_core(q, qa, k, ka, vt, t=512, g=512):
    B, S, D = q.shape
    kern = functools.partial(_fox_kernel, t=t, g=g)
    tile = pl.BlockSpec((1, t, LANES), lambda b, hp, i: (b, i, hp))
    full = pl.BlockSpec((1, S, LANES), lambda b, hp, i: (b, 0, hp))
    return pl.pallas_call(
        kern,
        grid=(B, D // LANES, S // t),
        in_specs=[tile, tile, full, full,
                  pl.BlockSpec((1, LANES, S), lambda b, hp, i: (b, hp, 0))],
        out_specs=tile,
        out_shape=jax.ShapeDtypeStruct((B, S, D), BF16),
        scratch_shapes=[
            pltpu.VMEM((1, 2 * t), F32), pltpu.VMEM((1, 2 * t), F32),
            pltpu.VMEM((LANES, 2 * t), F32),
            pltpu.VMEM((t, 2 * t), BF16), pltpu.VMEM((1, 2 * t), F32),
        ],
        compiler_params=_params("parallel", "parallel", "arbitrary"),
        name="fox_attention",
    )(q, qa, k, ka, vt)


def _fox_mixer(h, B, S, w_in, b_f, w_out, ln_g, ln_b):
    HD = ATT_HEADS * ATT_HEAD_DIM
    wq = w_in[:, :HD].astype(BF16)
    wk = w_in[:, HD:2 * HD].astype(BF16)
    wvt = w_in[:, 2 * HD:3 * HD].T.astype(BF16)
    wf = _pad_lanes(w_in[:, 3 * HD:]).astype(BF16)
    q, k, vt, f = _fox_proj(h, B, S, wq, wk, wvt, wf)
    qa, ka = _fox_cum(f.reshape(B, S, LANES), b_f, HD)
    o = _fox_attention_core(q.reshape(B, S, HD), qa, k.reshape(B, S, HD), ka, vt)
    return _outproj_ln(o.reshape(B * S, HD), w_out.astype(BF16), h, ln_g, ln_b)


def _moba_proj_kernel(x_ref, wq_ref, wk_ref, wvt_ref, q_ref, k_ref, vt_ref):
    xb = x_ref[...].astype(BF16)
    q_ref[...] = jnp.dot(xb, wq_ref[...], preferred_element_type=F32)
    k_ref[...] = jnp.dot(xb, wk_ref[...], preferred_element_type=F32).astype(BF16)
    vt_ref[0] = lax.dot_general(wvt_ref[...], xb, (((1,), (1,)), ((), ())),
                                preferred_element_type=F32).astype(BF16)


def _moba_proj(h, B, S, wq, wk, wvt, tm=512):
    T, D = h.shape
    HD = wq.shape[1]
    spt = S // tm
    return pl.pallas_call(
        _moba_proj_kernel,
        grid=(T // tm,),
        in_specs=[pl.BlockSpec((tm, D), lambda i: (i, 0)),
                  _const_spec((D, HD)), _const_spec((D, HD)), _const_spec((HD, D))],
        out_specs=[pl.BlockSpec((tm, HD), lambda i: (i, 0)),
                   pl.BlockSpec((tm, HD), lambda i: (i, 0)),
                   pl.BlockSpec((1, HD, tm), lambda i: (i // spt, 0, i % spt))],
        out_shape=[jax.ShapeDtypeStruct((T, HD), F32), jax.ShapeDtypeStruct((T, HD), BF16),
                   jax.ShapeDtypeStruct((B, HD, S), BF16)],
        compiler_params=_params("parallel"),
        name="moba_proj",
    )(h, wq, wk, wvt)


def _moba_kernel(qf_ref, k_ref, vt_ref, o_ref, m_ref, l_ref, acc_ref, p_ref, a_ref, km_ref, bias_ref,
                 *, nb, g):
    L = MOBA_BLOCK
    qt = pl.program_id(2)

    @pl.when(qt == 0)
    def _():
        for n in range(nb):
            kb = k_ref[0, n * L:(n + 1) * L, :].astype(F32)
            km_ref[n:n + 1, :] = jnp.sum(kb, 0, keepdims=True) * (1.0 / L)

    t = 2 * L
    masks = _head_masks(t, LANES)
    qf = qf_ref[0]
    qfcat = jnp.concatenate([jnp.where(masks[e], qf, jnp.zeros_like(qf)) for e in range(2)], axis=0)
    qcat = (qfcat * (ATT_SCALE * LOG2E)).astype(BF16)
    colq = lax.broadcasted_iota(jnp.int32, (1, 2 * t), 1) & (t - 1)
    second = colq >= L
    own = (2 * qt + second.astype(jnp.int32)).astype(F32)
    blk = lax.broadcasted_iota(jnp.int32, (nb, 2 * t), 0).astype(F32)

    gate = lax.dot_general(km_ref[...], qfcat, (((1,), (1,)), ((), ())), preferred_element_type=F32,
                           precision=lax.Precision.HIGHEST)
    gate = jnp.where(blk < own, gate, NEG_INF)
    bias = jnp.full((nb, 2 * t), NEG_INF, F32)
    for _ in range(MOBA_TOPK):
        mx = jnp.max(gate, 0, keepdims=True)
        first = jnp.min(jnp.where(gate == mx, blk, float(nb)), 0, keepdims=True)
        pick = jnp.logical_and(blk == first, mx > NEG_INF)
        bias = jnp.where(pick, 0.0, bias)
        gate = jnp.where(pick, NEG_INF, gate)
    bias_ref[...] = bias

    kpos = lax.broadcasted_iota(jnp.int32, (t, g), 0)
    qcol = lax.broadcasted_iota(jnp.int32, (t, g), 1)

    def own_adjust(c, s):
        s = jnp.where(kpos <= ((qcol + c * g) & (t - 1)), s, NEG_INF)
        past = jnp.where(second[:, c * g:(c + 1) * g], bias_ref[pl.ds(2 * qt, 1), c * g:(c + 1) * g], 0.0)
        return jnp.concatenate([s[0:L] + past, s[L:t]], axis=0)

    def keys(j):
        return k_ref[0, pl.ds(pl.multiple_of(j * t, t), t), :]

    def pv(j):
        _pv_update(p_ref, a_ref, vt_ref[0, :, pl.ds(pl.multiple_of(j * t, t), t)], acc_ref, t)

    acc_ref[...] = jnp.zeros_like(acc_ref)
    _qk_softmax(keys(qt), qcat, own_adjust, m_ref, l_ref, p_ref, a_ref, t, g, first=True)

    def past_tile(j, prev):
        def past_adjust(c, s):
            b0 = bias_ref[pl.ds(2 * j, 1), c * g:(c + 1) * g]
            b1 = bias_ref[pl.ds(2 * j + 1, 1), c * g:(c + 1) * g]
            return jnp.concatenate([s[0:L] + b0, s[L:t] + b1], axis=0)

        _qk_softmax(keys(j), qcat, past_adjust, m_ref, l_ref, p_ref, a_ref, t, g,
                    pending=lambda: pv(prev))

    last = _past_tiles_in_pairs(qt, past_tile)
    pv(last)
    _finish_pair(o_ref, l_ref, acc_ref, t)


def _moba_attention_core(qf, k, vt, g=512):
    B, S, D = qf.shape
    t = 2 * MOBA_BLOCK
    nb = S // MOBA_BLOCK
    tile = pl.BlockSpec((1, t, LANES), lambda b, hp, i: (b, i, hp))
    return pl.pallas_call(
        functools.partial(_moba_kernel, nb=nb, g=g),
        grid=(B, D // LANES, S // t),
        in_specs=[tile,
                  pl.BlockSpec((1, S, LANES), lambda b, hp, i: (b, 0, hp)),
                  pl.BlockSpec((1, LANES, S), lambda b, hp, i: (b, hp, 0))],
        out_specs=tile,
        out_shape=jax.ShapeDtypeStruct((B, S, D), BF16),
        scratch_shapes=[
            pltpu.VMEM((1, 2 * t), F32), pltpu.VMEM((1, 2 * t), F32),
            pltpu.VMEM((LANES, 2 * t), F32),
            pltpu.VMEM((t, 2 * t), BF16), pltpu.VMEM((1, 2 * t), F32),
            pltpu.VMEM((nb, LANES), F32), pltpu.VMEM((nb, 2 * t), F32),
        ],
        compiler_params=_params("parallel", "arbitrary", "arbitrary"),
        name="moba_attention",
    )(qf, k, vt)


def _moba_mixer(h, B, S, w_in, w_out, ln_g, ln_b):
    HD = ATT_HEADS * ATT_HEAD_DIM
    wq = w_in[:, :HD].astype(BF16)
    wk = w_in[:, HD:2 * HD].astype(BF16)
    wvt = w_in[:, 2 * HD:3 * HD].T.astype(BF16)
    qf, k, vt = _moba_proj(h, B, S, wq, wk, wvt)
    o = _moba_attention_core(qf.reshape(B, S, HD), k.reshape(B, S, HD), vt)
    return _outproj_ln(o.reshape(B * S, HD), w_out.astype(BF16), h, ln_g, ln_b)


def kernel(x, ffn_w_gate, ffn_w_up, ffn_w_down, ln_g, ln_b, ssm_w_in, ssm_conv_w, ssm_conv_b,
           ssm_dt_bias, ssm_a_log, ssm_d, ssm_norm_w, ssm_w_out, fox_w_in, fox_b_f, fox_w_out,
           moba_w_in, moba_w_out):
    B, S, D = x.shape
    h = x.reshape(B * S, D)
    for layer in range(DEPTH):
        kind, j = layer % 3, layer // 3
        h = _ffn_ln(h, ffn_w_gate[layer, 0].astype(BF16), ffn_w_up[layer, 0].astype(BF16),
                    ffn_w_down[layer, 0].astype(BF16), ln_g[layer, 0], ln_b[layer, 0])
        if kind == 0:
            h = _mamba2_mixer(h, B, S, ssm_w_in[j], ssm_conv_w[j], ssm_conv_b[j], ssm_dt_bias[j],
                              ssm_a_log[j], ssm_d[j], ssm_norm_w[j], ssm_w_out[j],
                              ln_g[layer, 1], ln_b[layer, 1])
        elif kind == 1:
            h = _fox_mixer(h, B, S, fox_w_in[j], fox_b_f[j], fox_w_out[j],
                           ln_g[layer, 1], ln_b[layer, 1])
        else:
            h = _moba_mixer(h, B, S, moba_w_in[j], moba_w_out[j], ln_g[layer, 1], ln_b[layer, 1])
        h = _ffn_ln(h, ffn_w_gate[layer, 1].astype(BF16), ffn_w_up[layer, 1].astype(BF16),
                    ffn_w_down[layer, 1].astype(BF16), ln_g[layer, 2], ln_b[layer, 2])
    return h.reshape(B, S, D)
```
